```python
import jax, jax.numpy as jnp
from jax import lax
import numpy as np

D_MODEL = 1024
BATCH = 2
SEQ = 16384
DEPTH = 4

GRID_W = 64
CTX_LEN = 256
EPS = 1e-6

GROUP_DIM = 64
CONV_GROUPS = 4
CONV_W = CONV_GROUPS * GROUP_DIM
CONV_K = 3
SGU_GROUPS = 4
SGU_W = SGU_GROUPS * GROUP_DIM
CHUNK = 128
NA_HEADS = 8
NA_HEAD_DIM = GROUP_DIM
NA_W = NA_HEADS * NA_HEAD_DIM
NA_KH = 8
NA_KW = 16

MIX_W = CONV_W + SGU_W + NA_W
IN_W = 3 * CONV_W + 2 * SGU_W + 3 * NA_W

PEER_HEADS = 8
PEER_NKEYS = 128
PEER_N = PEER_NKEYS * PEER_NKEYS
PEER_DKEY = 256
PEER_TOPK = 16
TOKEN_BLOCK = 128

kernel_name = "hybrid_conv_sgu_natten_peer_dit"


def rmsnorm(x, g):
    xf = x.astype(jnp.float32)
    y = xf * lax.rsqrt(jnp.mean(xf * xf, axis=-1, keepdims=True) + EPS)
    return (y * g.astype(jnp.float32)).astype(x.dtype)


def group_rmsnorm(y, g, groups):
    shp = y.shape
    y = rmsnorm(y.reshape(*shp[:-1], groups, shp[-1] // groups), g.reshape(groups, -1))
    return y.reshape(shp)


def split_proj(p):
    a = p[..., :3 * CONV_W]
    s = p[..., 3 * CONV_W:3 * CONV_W + 2 * SGU_W]
    qkv = p[..., 3 * CONV_W + 2 * SGU_W:]
    return a, s, qkv


def to_heads(qkv):
    b, t, _ = qkv.shape
    q, k, v = jnp.split(qkv, 3, axis=-1)
    return (q.reshape(b, t, NA_HEADS, NA_HEAD_DIM), k.reshape(b, t, NA_HEADS, NA_HEAD_DIM),
            v.reshape(b, t, NA_HEADS, NA_HEAD_DIM))


def short_conv_mixer(p, w):
    gate_b, gate_c, xin = jnp.split(p, 3, axis=-1)
    z = jnp.pad(gate_c * xin, ((0, 0), (1, 1), (0, 0)))
    conv = z[:, :-2] * w[0] + z[:, 1:-1] * w[1] + z[:, 2:] * w[2]
    return gate_b * conv


def spatial_gating_mixer(p, w_s, b_s, g_v):
    b, t, _ = p.shape
    u, v = jnp.split(jax.nn.gelu(p), 2, axis=-1)
    v = group_rmsnorm(v, g_v, SGU_GROUPS)
    v = v.reshape(b, t // CHUNK, CHUNK, SGU_GROUPS, GROUP_DIM)
    mixed = jnp.einsum('gpq,bnqgc->bnpgc', w_s, v) + b_s.T[None, None, :, :, None]
    return u * mixed.reshape(b, t, SGU_W)


def neighbourhood_attention(q, k, v, kc, vc, rpb):
    b, t, h, dh = q.shape
    rows = t // GRID_W
    kh = min(NA_KH, rows)
    scale = dh ** -0.5
    qg = q.reshape(b, rows, GRID_W, h, dh)
    kg = k.reshape(b, rows, GRID_W, h, dh)
    vg = v.reshape(b, rows, GRID_W, h, dh)
    cols = jnp.arange(GRID_W)
    col_start = jnp.clip(cols - NA_KW // 2, 0, GRID_W - NA_KW)
    col_idx = col_start[:, None] + jnp.arange(NA_KW)[None, :]
    rpb_c = rpb[:, :, col_idx - cols[:, None] + (NA_KW - 1)]

    def row_step(r):
        rs = jnp.clip(r - kh // 2, 0, rows - kh)
        q_r = lax.dynamic_index_in_dim(qg, r, axis=1, keepdims=False)
        k_n = lax.dynamic_slice_in_dim(kg, rs, kh, axis=1)[:, :, col_idx]
        v_n = lax.dynamic_slice_in_dim(vg, rs, kh, axis=1)[:, :, col_idx]
        bias = rpb_c[:, rs + jnp.arange(kh) - r + (NA_KH - 1)]
        s_loc = jnp.einsum('bqhd,biqjhd->bhqij', q_r, k_n) * scale
        s_loc = s_loc + jnp.transpose(bias, (0, 2, 1, 3))[None]
        s_ctx = jnp.einsum('bqhd,bkhd->bhqk', q_r, kc) * scale
        s = jnp.concatenate([s_loc.reshape(b, h, GRID_W, kh * NA_KW), s_ctx], axis=-1)
        pr = jax.nn.softmax(s.astype(jnp.float32), axis=-1).astype(v.dtype)
        p_loc = pr[..., :kh * NA_KW].reshape(b, h, GRID_W, kh, NA_KW)
        p_ctx = pr[..., kh * NA_KW:]
        return (jnp.einsum('bhqij,biqjhd->bqhd', p_loc, v_n)
                + jnp.einsum('bhqk,bkhd->bqhd', p_ctx, vc))

    out = lax.map(row_step, jnp.arange(rows))
    return jnp.transpose(out, (1, 0, 2, 3, 4)).reshape(b, t, h * dh)


def context_attention(qc, kc, vc):
    s = jnp.einsum('bqhd,bkhd->bhqk', qc, kc) * (NA_HEAD_DIM ** -0.5)
    pr = jax.nn.softmax(s.astype(jnp.float32), axis=-1).astype(vc.dtype)
    return jnp.einsum('bhqk,bkhd->bqhd', pr, vc).reshape(qc.shape[0], qc.shape[1], NA_W)


def merge_heads(y_conv, y_sgu, y_na, out_norm_g, w_out):
    y = jnp.concatenate([y_conv, y_sgu, y_na], axis=-1)
    return group_rmsnorm(y, out_norm_g, MIX_W // GROUP_DIM) @ w_out


def peer_ffn(h, w_q, sub_keys, u_tab, v_tab):
    n, d = h.shape
    blocks = h.reshape(n // TOKEN_BLOCK, TOKEN_BLOCK, d)

    def block_step(hb):
        q = (hb @ w_q).reshape(TOKEN_BLOCK, PEER_HEADS, 2, PEER_DKEY // 2)
        s = jnp.einsum('thpd,hpkd->thpk', q, sub_keys).astype(jnp.float32)
        top_s, top_i = lax.top_k(s, PEER_TOPK)
        cand_s = (top_s[:, :, 0, :, None] + top_s[:, :, 1, None, :]).reshape(TOKEN_BLOCK, PEER_HEADS, -1)
        cand_i = (top_i[:, :, 0, :, None] * PEER_NKEYS + top_i[:, :, 1, None, :]).reshape(TOKEN_BLOCK, PEER_HEADS, -1)
        best_s, best_pos = lax.top_k(cand_s, PEER_TOPK)
        expert = jnp.take_along_axis(cand_i, best_pos, axis=-1)
        g = jax.nn.softmax(best_s, axis=-1)
        act = jax.nn.gelu(jnp.einsum('thkd,td->thk', u_tab[expert], hb))
        wgt = g.astype(act.dtype) * act
        return jnp.einsum('thk,thkd->td', wgt, v_tab[expert])

    return lax.map(block_step, blocks).reshape(n, d)


def setup_inputs(seed: int = 0) -> dict:
    key = jax.random.key(seed)
    ks = jax.random.split(key, 21)
    D = D_MODEL

    def nrm(k, shape, s):
        return jax.random.normal(k, shape, jnp.float32) * s

    return {
        "x": nrm(ks[0], (BATCH, SEQ, D), 1.0),
        "c": nrm(ks[1], (BATCH, D), 1.0),
        "ctx": nrm(ks[2], (BATCH, CTX_LEN, D), 1.0),
        "c_ctx": nrm(ks[3], (D,), 1.0),
        "ada_w": nrm(ks[4], (DEPTH, D, 6 * D), 0.5 * D ** -0.5),
        "ada_b": nrm(ks[5], (DEPTH, 6 * D), 0.02),
        "norm1_g": 1.0 + nrm(ks[6], (DEPTH, D), 0.05),
        "w_in": nrm(ks[7], (DEPTH, D, IN_W), D ** -0.5),
        "conv_w": nrm(ks[8], (DEPTH, CONV_K, CONV_W), CONV_K ** -0.5),
        "sgu_w": nrm(ks[9], (DEPTH, SGU_GROUPS, CHUNK, CHUNK), CHUNK ** -0.5),
        "sgu_b": 1.0 + nrm(ks[10], (DEPTH, SGU_GROUPS, CHUNK), 0.1),
        "sgu_norm_g": 1.0 + nrm(ks[11], (DEPTH, SGU_W), 0.05),
        "na_rpb": nrm(ks[12], (DEPTH, NA_HEADS, 2 * NA_KH - 1, 2 * NA_KW - 1), 0.2),
        "out_norm_g": 1.0 + nrm(ks[13], (DEPTH, MIX_W), 0.05),
        "w_out": nrm(ks[14], (DEPTH, MIX_W, D), MIX_W ** -0.5),
        "norm2_g": 1.0 + nrm(ks[15], (DEPTH, D), 0.05),
        "peer_wq": nrm(ks[16], (DEPTH, D, PEER_HEADS * PEER_DKEY), D ** -0.5),
        "peer_subkeys": nrm(ks[17], (DEPTH, PEER_HEADS, 2, PEER_NKEYS, PEER_DKEY // 2), (PEER_DKEY // 2) ** -0.5),
        "peer_u": nrm(ks[18], (DEPTH, PEER_N, D), D ** -0.5),
        "peer_v": nrm(ks[19], (DEPTH, PEER_N, D), 0.5),
        "final_norm_g": 1.0 + nrm(ks[20], (D,), 0.05),
    }


def reference(x, c, ctx, c_ctx, ada_w, ada_b, norm1_g, w_in, conv_w, sgu_w, sgu_b, sgu_norm_g,
              na_rpb, out_norm_g, w_out, norm2_g, peer_wq, peer_subkeys, peer_u, peer_v,
              final_norm_g):
    b, t, d = x.shape
    xc = ctx
    for l in range(DEPTH):
        last = l == DEPTH - 1
        sh1, sc1, g1, sh2, sc2, g2 = jnp.split((jax.nn.silu(c) @ ada_w[l] + ada_b[l])[:, None, :], 6, axis=-1)
        csh1, csc1, cg1, csh2, csc2, cg2 = jnp.split(jax.nn.silu(c_ctx) @ ada_w[l] + ada_b[l], 6, axis=-1)

        h = rmsnorm(x, norm1_g[l]) * (1.0 + sc1) + sh1
        hc = rmsnorm(xc, norm1_g[l]) * (1.0 + csc1) + csh1
        a_l, s_l, qkv_l = split_proj(h @ w_in[l])
        a_c, s_c, qkv_c = split_proj(hc @ w_in[l])
        q, k, v = to_heads(qkv_l)
        qc, kc, vc = to_heads(qkv_c)
        y = merge_heads(short_conv_mixer(a_l, conv_w[l]),
                        spatial_gating_mixer(s_l, sgu_w[l], sgu_b[l], sgu_norm_g[l]),
                        neighbourhood_attention(q, k, v, kc, vc, na_rpb[l]),
                        out_norm_g[l], w_out[l])
        x = x + g1 * y
        h2 = rmsnorm(x, norm2_g[l]) * (1.0 + sc2) + sh2

        if last:
            x = x + g2 * peer_ffn(h2.reshape(-1, d), peer_wq[l], peer_subkeys[l], peer_u[l], peer_v[l]).reshape(x.shape)
        else:
            yc = merge_heads(short_conv_mixer(a_c, conv_w[l]),
                             spatial_gating_mixer(s_c, sgu_w[l], sgu_b[l], sgu_norm_g[l]),
                             context_attention(qc, kc, vc),
                             out_norm_g[l], w_out[l])
            xc = xc + cg1 * yc
            hc2 = rmsnorm(xc, norm2_g[l]) * (1.0 + csc2) + csh2
            f = peer_ffn(jnp.concatenate([h2.reshape(-1, d), hc2.reshape(-1, d)], axis=0),
                         peer_wq[l], peer_subkeys[l], peer_u[l], peer_v[l])
            n_lat = b * t
            x = x + g2 * f[:n_lat].reshape(x.shape)
            xc = xc + cg2 * f[n_lat:].reshape(xc.shape)
    return rmsnorm(x, final_norm_g)
```

```python
import functools

import numpy as np
import jax
import jax.numpy as jnp
from jax import lax
from jax.experimental import pallas as pl
from jax.experimental.pallas import tpu as pltpu

D_MODEL = 1024
GRID_W = 64
CTX_LEN = 256
EPS = 1e-6
GROUP_DIM = 64
CONV_W = 256
SGU_W = 256
SGU_GROUPS = 4
CHUNK = 128
NA_HEADS = 8
NA_W = 512
NA_KH = 8
NA_KW = 16
MIX_COLS = 3 * CONV_W + 2 * SGU_W
IN_W = MIX_COLS + 3 * NA_W
PEER_HEADS = 8
PEER_NKEYS = 128
PEER_TOPK = 16
PEER_DKEY = 256

LANES = 128
SUBLANES = 8
TOKEN_TILE = 512
QUERY_ROWS = 4
QUERY_BLOCK = QUERY_ROWS * GRID_W
KEY_ROWS = 12
KEY_BLOCK = KEY_ROWS * GRID_W
MASKED = -1e30
PEER_EXPERT_BLOCK = 1024
PEER_PAIR = 2 * PEER_NKEYS
VMEM_LIMIT = 56 * 1024 * 1024

MXU_DTYPE = jnp.bfloat16
F32 = jnp.float32
NT_DIMS = (((1,), (1,)), ((), ()))


def _cparams(sem):
    return pltpu.CompilerParams(dimension_semantics=sem, vmem_limit_bytes=VMEM_LIMIT)


def _rms(x, g):
    ms = jnp.mean(x * x, axis=-1, keepdims=True)
    return x * lax.rsqrt(ms + EPS) * g


def _group_rms128(y, g, lo):
    y2 = y * y
    s_lo = jnp.sum(jnp.where(lo, y2, 0.0), axis=-1, keepdims=True)
    s_hi = jnp.sum(jnp.where(lo, 0.0, y2), axis=-1, keepdims=True)
    inv = 1.0 / GROUP_DIM
    r = jnp.where(lo, lax.rsqrt(s_lo * inv + EPS), lax.rsqrt(s_hi * inv + EPS))
    return y * r * g


def _lo_mask():
    return lax.broadcasted_iota(jnp.int32, (1, LANES), 1) < GROUP_DIM


def _mod_row(mod_ref, tiles_per_batch):
    g = jnp.minimum(pl.program_id(0) // tiles_per_batch, 2)
    return mod_ref[pl.ds(g, 1), :]


def _adaln_kernel(c_ref, w_ref, b_ref, o_ref):
    cv = c_ref[...]
    s = cv * (1.0 / (1.0 + jnp.exp(-cv)))
    o_ref[...] = jnp.dot(s.astype(MXU_DTYPE), w_ref[...].astype(MXU_DTYPE),
                         preferred_element_type=F32) + b_ref[...]


def _adaln(cvec, ada_w, ada_b):
    depth, d, cols = ada_w.shape
    cb = 1536
    return pl.pallas_call(
        _adaln_kernel,
        grid=(depth, cols // cb),
        in_specs=[pl.BlockSpec((SUBLANES, d), lambda l, j: (0, 0)),
                  pl.BlockSpec((None, d, cb), lambda l, j: (l, 0, j)),
                  pl.BlockSpec((None, 1, cb), lambda l, j: (l, 0, j))],
        out_specs=pl.BlockSpec((None, SUBLANES, cb), lambda l, j: (l, 0, j)),
        out_shape=jax.ShapeDtypeStruct((depth, SUBLANES, cols), F32),
        compiler_params=_cparams(("parallel", "parallel")),
    )(cvec, ada_w, ada_b.reshape(depth, 1, cols))


def _proj_kernel(x_ref, mod_ref, n1g_ref, w_ref, sw_ref, sb_ref, sg_ref,
                 gb_ref, z_ref, ysgu_ref, qkv_ref, *, tiles_per_batch):
    d = D_MODEL
    mrow = _mod_row(mod_ref, tiles_per_batch)
    sh1, sc1 = mrow[:, 0:d], mrow[:, d:2 * d]
    h = _rms(x_ref[...], n1g_ref[...]) * (1.0 + sc1) + sh1
    p = jnp.dot(h.astype(MXU_DTYPE), w_ref[...], preferred_element_type=F32)
    gb_ref[...] = p[:, 0:CONV_W]
    z_ref[...] = p[:, CONV_W:2 * CONV_W] * p[:, 2 * CONV_W:3 * CONV_W]
    sg = jax.nn.gelu(p[:, 3 * CONV_W:MIX_COLS])
    u, v = sg[:, :SGU_W], sg[:, SGU_W:]
    lo = _lo_mask()
    vn = jnp.concatenate(
        [_group_rms128(v[:, :LANES], sg_ref[:, :LANES], lo),
         _group_rms128(v[:, LANES:], sg_ref[:, LANES:], lo)], axis=1).astype(MXU_DTYPE)
    grp = lax.broadcasted_iota(jnp.int32, (1, SGU_W), 1) // GROUP_DIM
    for c in range(TOKEN_TILE // CHUNK):
        rows = slice(c * CHUNK, (c + 1) * CHUNK)
        vc = vn[rows, :]
        mixed = jnp.zeros((CHUNK, SGU_W), F32)
        for gi in range(SGU_GROUPS):
            mg = jnp.dot(sw_ref[gi], vc, preferred_element_type=F32)
            mixed = jnp.where(grp == gi, mg, mixed)
        ysgu_ref[rows, :] = u[rows, :] * (mixed + sb_ref[...])
    qkv_ref[...] = p[:, MIX_COLS:].astype(qkv_ref.dtype)


def _proj(xa, mod, n1g, w_in, sgu_w, sgu_bias, sgu_g, tiles_per_batch):
    n, d = xa.shape
    tm = TOKEN_TILE
    row = lambda w: pl.BlockSpec((tm, w), lambda t: (t, 0))
    full = lambda a: pl.BlockSpec(a.shape, lambda t: (0,) * a.ndim)
    return pl.pallas_call(
        functools.partial(_proj_kernel, tiles_per_batch=tiles_per_batch),
        grid=(n // tm,),
        in_specs=[row(d), full(mod), full(n1g), full(w_in), full(sgu_w), full(sgu_bias), full(sgu_g)],
        out_specs=[row(CONV_W), row(CONV_W), row(SGU_W), row(3 * NA_W)],
        out_shape=[jax.ShapeDtypeStruct((n, CONV_W), F32), jax.ShapeDtypeStruct((n, CONV_W), F32),
                   jax.ShapeDtypeStruct((n, SGU_W), F32), jax.ShapeDtypeStruct((n, 3 * NA_W), MXU_DTYPE)],
        compiler_params=_cparams(("parallel",)),
    )(xa, mod, n1g, w_in, sgu_w, sgu_bias, sgu_g)


def _attn_kernel(q_ref, k_ref, v_ref, kc_ref, vc_ref, bias_ref, o_ref, *, rows):
    j = pl.program_id(2)
    kr0 = jnp.clip(QUERY_ROWS * j - NA_KH // 2, 0, rows - KEY_ROWS)
    start = pl.multiple_of(kr0 * GRID_W, GRID_W)
    kl = k_ref[pl.ds(start, KEY_BLOCK), :]
    vl = v_ref[pl.ds(start, KEY_BLOCK), :]
    kc, vc, q = kc_ref[...], vc_ref[...], q_ref[...]
    lo = _lo_mask()
    scale = GROUP_DIM ** -0.5
    o = jnp.zeros((QUERY_BLOCK, LANES), F32)
    for hh in range(2):
        hm = lo if hh == 0 else jnp.logical_not(lo)
        qh = jnp.where(hm, q, jnp.zeros_like(q)) * scale
        s_loc = lax.dot_general(qh, kl, NT_DIMS, preferred_element_type=F32) + bias_ref[0, hh]
        s_ctx = lax.dot_general(qh, kc, NT_DIMS, preferred_element_type=F32)
        m = jnp.maximum(jnp.max(s_loc, axis=-1, keepdims=True), jnp.max(s_ctx, axis=-1, keepdims=True))
        p_loc = jnp.exp(s_loc - m)
        p_ctx = jnp.exp(s_ctx - m)
        den = jnp.sum(p_loc, axis=-1, keepdims=True) + jnp.sum(p_ctx, axis=-1, keepdims=True)
        vlh = jnp.where(hm, vl, jnp.zeros_like(vl))
        vch = jnp.where(hm, vc, jnp.zeros_like(vc))
        oh = (jnp.dot(p_loc.astype(MXU_DTYPE), vlh, preferred_element_type=F32)
              + jnp.dot(p_ctx.astype(MXU_DTYPE), vch, preferred_element_type=F32))
        o = o + oh * (1.0 / den)
    o_ref[...] = o


def _attention(qkv, bias, batch, seq):
    n = qkv.shape[0]
    rows = seq // GRID_W
    nblk = rows // QUERY_ROWS
    ctx_q0 = batch * seq // QUERY_BLOCK
    heads_cols = NA_W // LANES

    def q_map(b, hp, j):
        return (jnp.where(j < nblk, b * nblk + j, ctx_q0 + b), hp)

    def variant(b, hp, j):
        v = jnp.where(j == 0, 0, jnp.where(j < nblk - 1, 1, jnp.where(j == nblk - 1, 2, 3)))
        return (v, hp, 0, 0)

    return pl.pallas_call(
        functools.partial(_attn_kernel, rows=rows),
        grid=(batch, heads_cols, nblk + 1),
        in_specs=[pl.BlockSpec((QUERY_BLOCK, LANES), q_map),
                  pl.BlockSpec((seq, LANES), lambda b, hp, j: (b, heads_cols + hp)),
                  pl.BlockSpec((seq, LANES), lambda b, hp, j: (b, 2 * heads_cols + hp)),
                  pl.BlockSpec((CTX_LEN, LANES), lambda b, hp, j: (ctx_q0 + b, heads_cols + hp)),
                  pl.BlockSpec((CTX_LEN, LANES), lambda b, hp, j: (ctx_q0 + b, 2 * heads_cols + hp)),
                  pl.BlockSpec((1, 2, QUERY_BLOCK, KEY_BLOCK), variant)],
        out_specs=pl.BlockSpec((QUERY_BLOCK, LANES), q_map),
        out_shape=jax.ShapeDtypeStruct((n, NA_W), F32),
        compiler_params=_cparams(("parallel", "parallel", "arbitrary")),
    )(qkv, qkv, qkv, qkv, qkv, bias)


def _bias_tables(rows):
    kh = min(NA_KH, rows)
    qr = np.arange(QUERY_ROWS)[:, None, None, None]
    qc = np.arange(GRID_W)[None, :, None, None]
    kr = np.arange(KEY_ROWS)[None, None, :, None]
    kc = np.arange(GRID_W)[None, None, None, :]
    ii, jj, ok = [], [], []
    for r0, kr0 in ((0, 0), (QUERY_ROWS, 0), (rows - QUERY_ROWS, rows - KEY_ROWS)):
        r = r0 + qr
        rs = np.clip(r - kh // 2, 0, rows - kh)
        krow = kr0 + kr
        cs = np.clip(qc - NA_KW // 2, 0, GRID_W - NA_KW)
        valid = (krow >= rs) & (krow < rs + kh) & (kc >= cs) & (kc < cs + NA_KW)
        i_idx = np.clip(krow - r + (NA_KH - 1), 0, 2 * NA_KH - 2)
        j_idx = np.clip(kc - qc + (NA_KW - 1), 0, 2 * NA_KW - 2)
        shape = (QUERY_ROWS, GRID_W, KEY_ROWS, GRID_W)
        ii.append(np.broadcast_to(i_idx, shape).reshape(QUERY_BLOCK, KEY_BLOCK))
        jj.append(np.broadcast_to(j_idx, shape).reshape(QUERY_BLOCK, KEY_BLOCK))
        ok.append(np.broadcast_to(valid, shape).reshape(QUERY_BLOCK, KEY_BLOCK))
    ii.append(np.zeros_like(ii[0]))
    jj.append(np.zeros_like(jj[0]))
    ok.append(np.zeros_like(ok[0]))
    return np.stack(ii), np.stack(jj), np.stack(ok)


def _build_bias(rpb, tables):
    ii, jj, ok = tables
    g = rpb[:, ii, jj]
    return jnp.transpose(jnp.where(ok[None], g, MASKED), (1, 0, 2, 3))


def _merge_kernel(gb_ref, z_ref, zp_ref, zn_ref, ysgu_ref, yna_ref, x_ref, mod_ref, cw_ref, og_ref,
                  w_ref, n2g_ref, xo_ref, h2_ref, *, tiles_per_batch, seq, n_latent):
    d = D_MODEL
    tm = TOKEN_TILE
    t = pl.program_id(0)
    mrow = _mod_row(mod_ref, tiles_per_batch)
    g1, sh2, sc2 = mrow[:, 2 * d:3 * d], mrow[:, 3 * d:4 * d], mrow[:, 4 * d:5 * d]
    z = z_ref[...]
    ridx = lax.broadcasted_iota(jnp.int32, (tm, 1), 0)
    grow = t * tm + ridx
    seqlen = jnp.where(t * tm >= n_latent, CTX_LEN, seq)
    pos = grow & (seqlen - 1)
    z_prev = jnp.where(ridx == 0, zp_ref[SUBLANES - 1:SUBLANES, :], pltpu.roll(z, 1, 0))
    z_next = jnp.where(ridx == tm - 1, zn_ref[0:1, :], pltpu.roll(z, tm - 1, 0))
    z_prev = jnp.where(pos == 0, 0.0, z_prev)
    z_next = jnp.where(pos == seqlen - 1, 0.0, z_next)
    conv = z_prev * cw_ref[0:1, :] + z * cw_ref[1:2, :] + z_next * cw_ref[2:3, :]
    y_conv = gb_ref[...] * conv
    lo = _lo_mask()
    blocks = []
    for src, width in ((y_conv, CONV_W), (ysgu_ref[...], SGU_W), (yna_ref[...], NA_W)):
        for c in range(width // LANES):
            off = len(blocks) * LANES
            blocks.append(_group_rms128(src[:, c * LANES:(c + 1) * LANES], og_ref[:, off:off + LANES], lo)
                          .astype(MXU_DTYPE))
    y = jnp.dot(jnp.concatenate(blocks, axis=1), w_ref[...], preferred_element_type=F32)
    xm = x_ref[...] + g1 * y
    xo_ref[...] = xm
    h2_ref[...] = (_rms(xm, n2g_ref[...]) * (1.0 + sc2) + sh2).astype(h2_ref.dtype)


def _merge(gb, z, ysgu, yna, xa, mod, conv_w, out_g, w_out, n2g, tiles_per_batch, seq, n_latent):
    n, d = xa.shape
    tm = TOKEN_TILE
    hb = tm // SUBLANES
    nb8 = n // SUBLANES
    row = lambda w: pl.BlockSpec((tm, w), lambda t: (t, 0))
    full = lambda a: pl.BlockSpec(a.shape, lambda t: (0,) * a.ndim)
    halo_prev = pl.BlockSpec((SUBLANES, CONV_W), lambda t: (jnp.maximum(t * hb - 1, 0), 0))
    halo_next = pl.BlockSpec((SUBLANES, CONV_W), lambda t: (jnp.minimum((t + 1) * hb, nb8 - 1), 0))
    return pl.pallas_call(
        functools.partial(_merge_kernel, tiles_per_batch=tiles_per_batch, seq=seq, n_latent=n_latent),
        grid=(n // tm,),
        in_specs=[row(CONV_W), row(CONV_W), halo_prev, halo_next, row(SGU_W), row(NA_W), row(d),
                  full(mod), full(conv_w), full(out_g), full(w_out), full(n2g)],
        out_specs=[row(d), row(d)],
        out_shape=[jax.ShapeDtypeStruct((n, d), F32), jax.ShapeDtypeStruct((n, d), MXU_DTYPE)],
        compiler_params=_cparams(("parallel",)),
    )(gb, z, z, z, ysgu, yna, xa, mod, conv_w, out_g, w_out, n2g)


def _sort_desc(vals):
    n = len(vals)
    size = 2
    while size <= n:
        k = size // 2
        while k >= 1:
            for i in range(n):
                p = i ^ k
                if p > i:
                    hi, lo = jnp.maximum(vals[i], vals[p]), jnp.minimum(vals[i], vals[p])
                    vals[i], vals[p] = (hi, lo) if (i & size) == 0 else (lo, hi)
            k //= 2
        size *= 2
    return vals


def _merge_bitonic_desc(vals):
    n = len(vals)
    k = n // 2
    while k >= 1:
        for i in range(n):
            if (i & k) == 0:
                hi, lo = jnp.maximum(vals[i], vals[i + k]), jnp.minimum(vals[i], vals[i + k])
                vals[i], vals[i + k] = hi, lo
        k //= 2
    return vals


def _top_merge(a, b):
    n = len(a)
    return _merge_bitonic_desc([jnp.maximum(a[i], b[n - 1 - i]) for i in range(n)])


def _top16_over_keys(tiles):
    vals = _sort_desc(list(tiles))
    for shift in (4, 2, 1):
        vals = _top_merge(vals, [pltpu.roll(v, shift, 0) for v in vals])
    return vals


def _pair_threshold(a, b):
    k = PEER_TOPK
    rows = [[a[i] + b[j] for j in range(k // (i + 1))] for i in range(k)]
    neg = jnp.full(a[0].shape, -jnp.inf, F32)
    pad = lambda vals: vals + [neg] * (k - len(vals))
    l1 = _sort_desc(pad(rows[1] + rows[2]))
    l2 = _sort_desc(pad(rows[3] + rows[4] + rows[5] + rows[6] + rows[7]))
    l3 = _sort_desc(pad([v for r in rows[8:] for v in r]))
    return _top_merge(_top_merge(rows[0], l1), _top_merge(l2, l3))


def _peer_kernel(h2_ref, x_ref, mod_ref, wq_ref, sk_ref, u_ref, vt_ref, o_ref,
                 s_sc, e_sc, tau_sc, qt_sc, act_sc, w_sc, acc_sc, *, tiles_per_batch):
    d = D_MODEL
    tm = TOKEN_TILE
    nk = PEER_NKEYS
    key_tiles = nk // SUBLANES
    lane_tiles = tm // LANES
    e = pl.program_id(1)
    keys_per_step = PEER_EXPERT_BLOCK // nk

    @pl.when(e == 0)
    def _scores_and_thresholds():
        q_t = lax.dot_general(wq_ref[...], h2_ref[...], NT_DIMS, preferred_element_type=F32)
        qt_sc[...] = q_t.astype(qt_sc.dtype)
        for hp in range(2 * PEER_HEADS):
            s_t = jnp.dot(sk_ref[hp], qt_sc[hp * nk:(hp + 1) * nk, :], preferred_element_type=F32)
            for lt in range(lane_tiles):
                s_sc[hp, lt] = s_t[:, lt * LANES:(lt + 1) * LANES]
        acc_sc[...] = jnp.zeros_like(acc_sc)

        def head_tile(it, carry):
            h = it // lane_tiles
            lt = it % lane_tiles
            rows = lambda v: slice(v * SUBLANES, (v + 1) * SUBLANES)
            s0 = [s_sc[2 * h, lt, rows(v), :] for v in range(key_tiles)]
            s1 = [s_sc[2 * h + 1, lt, rows(v), :] for v in range(key_tiles)]
            a = _top16_over_keys(s0)
            b = _top16_over_keys(s1)
            best = _pair_threshold(a, b)
            tau_sc[h, lt] = best[PEER_TOPK - 1]
            z = jnp.exp(best[0] - best[0])
            for kk in range(1, PEER_TOPK):
                z = z + jnp.exp(best[kk] - best[0])
            rz = 1.0 / z
            for v in range(key_tiles):
                e_sc[2 * h, lt, rows(v), :] = jnp.exp(s0[v] - a[0])
                e_sc[2 * h + 1, lt, rows(v), :] = jnp.exp(s1[v] - b[0]) * rz
            return carry

        lax.fori_loop(0, PEER_HEADS * lane_tiles, head_tile, 0)

    def pair_step(ip, carry):
        r0 = pl.multiple_of(ip * PEER_PAIR, PEER_PAIR)
        act_sc[...] = lax.dot_general(u_ref[pl.ds(r0, PEER_PAIR), :], h2_ref[...], NT_DIMS,
                                      preferred_element_type=F32)

        def key_tile(k, c2):
            ii = k // lane_tiles
            lt = k % lane_tiles
            lanes = pl.ds(pl.multiple_of(lt * LANES, LANES), LANES)
            i = e * keys_per_step + ip * 2 + ii
            gate = [jnp.zeros((SUBLANES, LANES), F32) for _ in range(key_tiles)]
            for h in range(PEER_HEADS):
                s0 = jnp.broadcast_to(s_sc[2 * h, lt, pl.ds(i, 1), :], (SUBLANES, LANES))
                e0 = jnp.broadcast_to(e_sc[2 * h, lt, pl.ds(i, 1), :], (SUBLANES, LANES))
                tau = tau_sc[h, lt]
                for jt in range(key_tiles):
                    rows = slice(jt * SUBLANES, (jt + 1) * SUBLANES)
                    s1 = s_sc[2 * h + 1, lt, rows, :]
                    e1 = e_sc[2 * h + 1, lt, rows, :]
                    gate[jt] = gate[jt] + jnp.where(s0 + s1 >= tau, e0 * e1, 0.0)
            base = pl.multiple_of(ii * nk, nk)
            for jp in range(key_tiles // 2):
                rows = pl.ds(base + jp * 2 * SUBLANES, 2 * SUBLANES)
                g2 = jnp.concatenate([gate[2 * jp], gate[2 * jp + 1]], axis=0)
                w_sc[rows, lanes] = (g2 * jax.nn.gelu(act_sc[rows, lanes])).astype(w_sc.dtype)
            return c2

        lax.fori_loop(0, 2 * lane_tiles, key_tile, 0)
        acc_sc[...] += jnp.dot(vt_ref[:, pl.ds(r0, PEER_PAIR)], w_sc[...], preferred_element_type=F32)
        return carry

    lax.fori_loop(0, PEER_EXPERT_BLOCK // PEER_PAIR, pair_step, 0)

    @pl.when(e == pl.num_programs(1) - 1)
    def _residual():
        g2 = _mod_row(mod_ref, tiles_per_batch)[:, 5 * d:6 * d]
        o_ref[...] = x_ref[...] + g2 * acc_sc[...].T


def _peer(h2, xm, mod, wq_t, subkeys, u_tab, vt_tab, tiles_per_batch):
    n, d = xm.shape
    tm = TOKEN_TILE
    n_exp = u_tab.shape[0]
    eb = PEER_EXPERT_BLOCK
    full = lambda a: pl.BlockSpec(a.shape, lambda t, e: (0,) * a.ndim)
    row = pl.BlockSpec((tm, d), lambda t, e: (t, 0))
    hp2 = 2 * PEER_HEADS
    return pl.pallas_call(
        functools.partial(_peer_kernel, tiles_per_batch=tiles_per_batch),
        grid=(n // tm, n_exp // eb),
        in_specs=[row, row, full(mod), full(wq_t), full(subkeys),
                  pl.BlockSpec((eb, d), lambda t, e: (e, 0)),
                  pl.BlockSpec((d, eb), lambda t, e: (0, e))],
        out_specs=row,
        out_shape=jax.ShapeDtypeStruct((n, d), F32),
        scratch_shapes=[pltpu.VMEM((hp2, tm // LANES, PEER_NKEYS, LANES), F32),
                        pltpu.VMEM((hp2, tm // LANES, PEER_NKEYS, LANES), F32),
                        pltpu.VMEM((PEER_HEADS, tm // LANES, SUBLANES, LANES), F32),
                        pltpu.VMEM((hp2 * PEER_NKEYS, tm), MXU_DTYPE),
                        pltpu.VMEM((PEER_PAIR, tm), F32),
                        pltpu.VMEM((PEER_PAIR, tm), MXU_DTYPE),
                        pltpu.VMEM((d, tm), F32)],
        compiler_params=_cparams(("parallel", "arbitrary")),
    )(h2, xm, mod, wq_t, subkeys, u_tab, vt_tab)


def _final_kernel(x_ref, g_ref, o_ref):
    o_ref[...] = _rms(x_ref[...], g_ref[...])


def _final_norm(xa, g, n_latent):
    d = xa.shape[1]
    tm = TOKEN_TILE
    return pl.pallas_call(
        _final_kernel,
        grid=(n_latent // tm,),
        in_specs=[pl.BlockSpec((tm, d), lambda t: (t, 0)), pl.BlockSpec((1, d), lambda t: (0, 0))],
        out_specs=pl.BlockSpec((tm, d), lambda t: (t, 0)),
        out_shape=jax.ShapeDtypeStruct((n_latent, d), F32),
        compiler_params=_cparams(("parallel",)),
    )(xa, g)


def kernel(x, c, ctx, c_ctx, ada_w, ada_b, norm1_g, w_in, conv_w, sgu_w, sgu_b, sgu_norm_g, na_rpb,
           out_norm_g, w_out, norm2_g, peer_wq, peer_subkeys, peer_u, peer_v, final_norm_g):
    batch, seq, d = x.shape
    depth = ada_w.shape[0]
    assert d == D_MODEL and batch == 2 and ctx.shape[1] == CTX_LEN
    assert seq % TOKEN_TILE == 0 and batch * CTX_LEN == TOKEN_TILE
    rows = seq // GRID_W
    assert rows % QUERY_ROWS == 0 and rows >= KEY_ROWS + QUERY_ROWS
    n_latent = batch * seq
    tiles_per_batch = seq // TOKEN_TILE

    xa = jnp.concatenate([x.reshape(n_latent, d), ctx.reshape(batch * CTX_LEN, d)], axis=0)
    cvec = jnp.zeros((SUBLANES, d), F32).at[0:batch].set(c).at[batch].set(c_ctx)
    mods = _adaln(cvec, ada_w, ada_b)
    tables = _bias_tables(rows)

    mx = MXU_DTYPE
    w_in_c = w_in.astype(mx)
    w_out_c = w_out.astype(mx)
    sgu_w_c = sgu_w.astype(mx)
    sgu_bias = jnp.repeat(jnp.swapaxes(sgu_b, 1, 2), GROUP_DIM, axis=2)
    wq_t = jnp.swapaxes(peer_wq, 1, 2).astype(mx)
    subkeys = peer_subkeys.reshape(depth, 2 * PEER_HEADS, PEER_NKEYS, PEER_DKEY // 2).astype(mx)
    u_c = peer_u.astype(mx)
    vt_c = jnp.swapaxes(peer_v.astype(mx), 1, 2)

    for l in range(depth):
        mod = mods[l]
        gb, z, ysgu, qkv = _proj(xa, mod, norm1_g[l][None], w_in_c[l], sgu_w_c[l], sgu_bias[l],
                                 sgu_norm_g[l][None], tiles_per_batch)
        yna = _attention(qkv, _build_bias(na_rpb[l], tables), batch, seq)
        xm, h2 = _merge(gb, z, ysgu, yna, xa, mod, conv_w[l], out_norm_g[l][None], w_out_c[l],
                        norm2_g[l][None], tiles_per_batch, seq, n_latent)
        xa = _peer(h2, xm, mod, wq_t[l], subkeys[l], u_c[l], vt_c[l], tiles_per_batch)
    return _final_norm(xa, final_norm_g[None], n_latent).reshape(batch, seq, d)
```

```python
import functools

import numpy as np
import jax
import jax.numpy as jnp
from jax import lax
from jax.experimental import pallas as pl
from jax.experimental.pallas import tpu as pltpu

D_MODEL = 1024
GRID_W = 64
CTX_LEN = 256
EPS = 1e-6
GROUP_DIM = 64
CONV_W = 256
SGU_W = 256
SGU_GROUPS = 4
CHUNK = 128
NA_HEADS = 8
NA_W = 512
NA_KH = 8
NA_KW = 16
MIX_COLS = 3 * CONV_W + 2 * SGU_W
IN_W = MIX_COLS + 3 * NA_W
PEER_HEADS = 8
PEER_NKEYS = 128
PEER_TOPK = 16
PEER_DKEY = 256

LANES = 128
SUBLANES = 8
TOKEN_TILE = 512
QUERY_ROWS = 4
QUERY_BLOCK = QUERY_ROWS * GRID_W
KEY_ROWS = 12
KEY_BLOCK = KEY_ROWS * GRID_W
MASKED = -1e30
PEER_EXPERT_BLOCK = 1024
PEER_PAIR = 2 * PEER_NKEYS
PEER_KEYS_PER_PASS = 2
VMEM_LIMIT = 56 * 1024 * 1024
PEER_SCHEDULER_FLAGS = None

MXU_DTYPE = jnp.bfloat16
GATE_DTYPE = jnp.bfloat16
F32 = jnp.float32
NT_DIMS = (((1,), (1,)), ((), ()))


def _cparams(sem, flags=None):
    return pltpu.CompilerParams(dimension_semantics=sem, vmem_limit_bytes=VMEM_LIMIT, flags=flags)


def _rms(x, g):
    ms = jnp.mean(x * x, axis=-1, keepdims=True)
    return x * lax.rsqrt(ms + EPS) * g


def _group_rms128(y, g, lo):
    y2 = y * y
    s_lo = jnp.sum(jnp.where(lo, y2, 0.0), axis=-1, keepdims=True)
    s_hi = jnp.sum(jnp.where(lo, 0.0, y2), axis=-1, keepdims=True)
    inv = 1.0 / GROUP_DIM
    r = jnp.where(lo, lax.rsqrt(s_lo * inv + EPS), lax.rsqrt(s_hi * inv + EPS))
    return y * r * g


def _lo_mask():
    return lax.broadcasted_iota(jnp.int32, (1, LANES), 1) < GROUP_DIM


def _mod_row(mod_ref, tiles_per_batch):
    g = jnp.minimum(pl.program_id(0) // tiles_per_batch, 2)
    return mod_ref[pl.ds(g, 1), :]


def _adaln_kernel(c_ref, w_ref, b_ref, o_ref):
    cv = c_ref[...]
    s = cv * (1.0 / (1.0 + jnp.exp(-cv)))
    o_ref[...] = jnp.dot(s.astype(MXU_DTYPE), w_ref[...].astype(MXU_DTYPE),
                         preferred_element_type=F32) + b_ref[...]


def _adaln(cvec, ada_w, ada_b):
    depth, d, cols = ada_w.shape
    cb = 1536
    return pl.pallas_call(
        _adaln_kernel,
        grid=(depth, cols // cb),
        in_specs=[pl.BlockSpec((SUBLANES, d), lambda l, j: (0, 0)),
                  pl.BlockSpec((None, d, cb), lambda l, j: (l, 0, j)),
                  pl.BlockSpec((None, 1, cb), lambda l, j: (l, 0, j))],
        out_specs=pl.BlockSpec((None, SUBLANES, cb), lambda l, j: (l, 0, j)),
        out_shape=jax.ShapeDtypeStruct((depth, SUBLANES, cols), F32),
        compiler_params=_cparams(("parallel", "parallel")),
    )(cvec, ada_w, ada_b.reshape(depth, 1, cols))


def _proj_kernel(x_ref, mod_ref, n1g_ref, w_ref, sw_ref, sb_ref, sg_ref,
                 gb_ref, z_ref, ysgu_ref, qkv_ref, *, tiles_per_batch):
    d = D_MODEL
    mrow = _mod_row(mod_ref, tiles_per_batch)
    sh1, sc1 = mrow[:, 0:d], mrow[:, d:2 * d]
    h = _rms(x_ref[...], n1g_ref[...]) * (1.0 + sc1) + sh1
    p = jnp.dot(h.astype(MXU_DTYPE), w_ref[...], preferred_element_type=F32)
    gb_ref[...] = p[:, 0:CONV_W]
    z_ref[...] = p[:, CONV_W:2 * CONV_W] * p[:, 2 * CONV_W:3 * CONV_W]
    sg = jax.nn.gelu(p[:, 3 * CONV_W:MIX_COLS])
    u, v = sg[:, :SGU_W], sg[:, SGU_W:]
    lo = _lo_mask()
    vn = jnp.concatenate(
        [_group_rms128(v[:, :LANES], sg_ref[:, :LANES], lo),
         _group_rms128(v[:, LANES:], sg_ref[:, LANES:], lo)], axis=1).astype(MXU_DTYPE)
    grp = lax.broadcasted_iota(jnp.int32, (1, SGU_W), 1) // GROUP_DIM
    for c in range(TOKEN_TILE // CHUNK):
        rows = slice(c * CHUNK, (c + 1) * CHUNK)
        vc = vn[rows, :]
        mixed = jnp.zeros((CHUNK, SGU_W), F32)
        for gi in range(SGU_GROUPS):
            mg = jnp.dot(sw_ref[gi], vc, preferred_element_type=F32)
            mixed = jnp.where(grp == gi, mg, mixed)
        ysgu_ref[rows, :] = u[rows, :] * (mixed + sb_ref[...])
    qkv_ref[...] = p[:, MIX_COLS:].astype(qkv_ref.dtype)


def _proj(xa, mod, n1g, w_in, sgu_w, sgu_bias, sgu_g, tiles_per_batch):
    n, d = xa.shape
    tm = TOKEN_TILE
    row = lambda w: pl.BlockSpec((tm, w), lambda t: (t, 0))
    full = lambda a: pl.BlockSpec(a.shape, lambda t: (0,) * a.ndim)
    return pl.pallas_call(
        functools.partial(_proj_kernel, tiles_per_batch=tiles_per_batch),
        grid=(n // tm,),
        in_specs=[row(d), full(mod), full(n1g), full(w_in), full(sgu_w), full(sgu_bias), full(sgu_g)],
        out_specs=[row(CONV_W), row(CONV_W), row(SGU_W), row(3 * NA_W)],
        out_shape=[jax.ShapeDtypeStruct((n, CONV_W), F32), jax.ShapeDtypeStruct((n, CONV_W), F32),
                   jax.ShapeDtypeStruct((n, SGU_W), F32), jax.ShapeDtypeStruct((n, 3 * NA_W), MXU_DTYPE)],
        compiler_params=_cparams(("parallel",)),
    )(xa, mod, n1g, w_in, sgu_w, sgu_bias, sgu_g)


def _attn_kernel(q_ref, k_ref, v_ref, kc_ref, vc_ref, bias_ref, o_ref, *, rows):
    j = pl.program_id(2)
    kr0 = jnp.clip(QUERY_ROWS * j - NA_KH // 2, 0, rows - KEY_ROWS)
    start = pl.multiple_of(kr0 * GRID_W, GRID_W)
    kl = k_ref[pl.ds(start, KEY_BLOCK), :]
    vl = v_ref[pl.ds(start, KEY_BLOCK), :]
    kc, vc, q = kc_ref[...], vc_ref[...], q_ref[...]
    lo = _lo_mask()
    scale = GROUP_DIM ** -0.5
    o = jnp.zeros((QUERY_BLOCK, LANES), F32)
    for hh in range(2):
        hm = lo if hh == 0 else jnp.logical_not(lo)
        qh = jnp.where(hm, q, jnp.zeros_like(q)) * scale
        s_loc = lax.dot_general(qh, kl, NT_DIMS, preferred_element_type=F32) + bias_ref[0, hh]
        s_ctx = lax.dot_general(qh, kc, NT_DIMS, preferred_element_type=F32)
        m = jnp.maximum(jnp.max(s_loc, axis=-1, keepdims=True), jnp.max(s_ctx, axis=-1, keepdims=True))
        p_loc = jnp.exp(s_loc - m)
        p_ctx = jnp.exp(s_ctx - m)
        den = jnp.sum(p_loc, axis=-1, keepdims=True) + jnp.sum(p_ctx, axis=-1, keepdims=True)
        vlh = jnp.where(hm, vl, jnp.zeros_like(vl))
        vch = jnp.where(hm, vc, jnp.zeros_like(vc))
        oh = (jnp.dot(p_loc.astype(MXU_DTYPE), vlh, preferred_element_type=F32)
              + jnp.dot(p_ctx.astype(MXU_DTYPE), vch, preferred_element_type=F32))
        o = o + oh * (1.0 / den)
    o_ref[...] = o


def _attention(qkv, bias, batch, seq):
    n = qkv.shape[0]
    rows = seq // GRID_W
    nblk = rows // QUERY_ROWS
    ctx_q0 = batch * seq // QUERY_BLOCK
    heads_cols = NA_W // LANES

    def q_map(b, hp, j):
        return (jnp.where(j < nblk, b * nblk + j, ctx_q0 + b), hp)

    def variant(b, hp, j):
        v = jnp.where(j == 0, 0, jnp.where(j < nblk - 1, 1, jnp.where(j == nblk - 1, 2, 3)))
        return (v, hp, 0, 0)

    return pl.pallas_call(
        functools.partial(_attn_kernel, rows=rows),
        grid=(batch, heads_cols, nblk + 1),
        in_specs=[pl.BlockSpec((QUERY_BLOCK, LANES), q_map),
                  pl.BlockSpec((seq, LANES), lambda b, hp, j: (b, heads_cols + hp)),
                  pl.BlockSpec((seq, LANES), lambda b, hp, j: (b, 2 * heads_cols + hp)),
                  pl.BlockSpec((CTX_LEN, LANES), lambda b, hp, j: (ctx_q0 + b, heads_cols + hp)),
                  pl.BlockSpec((CTX_LEN, LANES), lambda b, hp, j: (ctx_q0 + b, 2 * heads_cols + hp)),
                  pl.BlockSpec((1, 2, QUERY_BLOCK, KEY_BLOCK), variant)],
        out_specs=pl.BlockSpec((QUERY_BLOCK, LANES), q_map),
        out_shape=jax.ShapeDtypeStruct((n, NA_W), F32),
        compiler_params=_cparams(("parallel", "parallel", "arbitrary")),
    )(qkv, qkv, qkv, qkv, qkv, bias)


def _bias_block_index(rows):
    kh = min(NA_KH, rows)
    idx = np.full((4, QUERY_ROWS, KEY_ROWS), 2 * NA_KH - 1, np.int32)
    for v, (r0, kr0) in enumerate(((0, 0), (QUERY_ROWS, 0), (rows - QUERY_ROWS, rows - KEY_ROWS))):
        for qr in range(QUERY_ROWS):
            r = r0 + qr
            rs = min(max(r - kh // 2, 0), rows - kh)
            for kr in range(KEY_ROWS):
                if rs <= kr0 + kr < rs + kh:
                    idx[v, qr, kr] = kr0 + kr - r + NA_KH - 1
    return idx


def _build_bias(rpb, block_index):
    heads = rpb.shape[0]
    pad = GRID_W - NA_KW
    padded = jnp.pad(rpb, ((0, 0), (0, 0), (pad, pad)), constant_values=MASKED)
    toe = jnp.stack([padded[:, :, GRID_W - 1 - qc:2 * GRID_W - 1 - qc] for qc in range(GRID_W)], axis=2)
    qc = np.arange(GRID_W)[:, None]
    kc = np.arange(GRID_W)[None, :]
    cs = np.clip(qc - NA_KW // 2, 0, GRID_W - NA_KW)
    toe = jnp.where((kc >= cs) & (kc < cs + NA_KW), toe, MASKED)
    toe = jnp.concatenate([toe, jnp.full((heads, 1, GRID_W, GRID_W), MASKED, toe.dtype)], axis=1)
    flat = [int(a) for a in block_index.reshape(-1)]
    blocks = jnp.stack([toe[:, a] for a in flat], axis=1)
    blocks = blocks.reshape(heads, 4, QUERY_ROWS, KEY_ROWS, GRID_W, GRID_W)
    return jnp.transpose(blocks, (1, 0, 2, 4, 3, 5)).reshape(4, heads, QUERY_BLOCK, KEY_BLOCK)


def _merge_kernel(gb_ref, z_ref, zp_ref, zn_ref, ysgu_ref, yna_ref, x_ref, mod_ref, cw_ref, og_ref,
                  w_ref, n2g_ref, xo_ref, h2_ref, *, tiles_per_batch, seq, n_latent):
    d = D_MODEL
    tm = TOKEN_TILE
    t = pl.program_id(0)
    mrow = _mod_row(mod_ref, tiles_per_batch)
    g1, sh2, sc2 = mrow[:, 2 * d:3 * d], mrow[:, 3 * d:4 * d], mrow[:, 4 * d:5 * d]
    z = z_ref[...]
    ridx = lax.broadcasted_iota(jnp.int32, (tm, 1), 0)
    grow = t * tm + ridx
    seqlen = jnp.where(t * tm >= n_latent, CTX_LEN, seq)
    pos = grow & (seqlen - 1)
    z_prev = jnp.where(ridx == 0, zp_ref[SUBLANES - 1:SUBLANES, :], pltpu.roll(z, 1, 0))
    z_next = jnp.where(ridx == tm - 1, zn_ref[0:1, :], pltpu.roll(z, tm - 1, 0))
    z_prev = jnp.where(pos == 0, 0.0, z_prev)
    z_next = jnp.where(pos == seqlen - 1, 0.0, z_next)
    conv = z_prev * cw_ref[0:1, :] + z * cw_ref[1:2, :] + z_next * cw_ref[2:3, :]
    y_conv = gb_ref[...] * conv
    lo = _lo_mask()
    blocks = []
    for src, width in ((y_conv, CONV_W), (ysgu_ref[...], SGU_W), (yna_ref[...], NA_W)):
        for c in range(width // LANES):
            off = len(blocks) * LANES
            blocks.append(_group_rms128(src[:, c * LANES:(c + 1) * LANES], og_ref[:, off:off + LANES], lo)
                          .astype(MXU_DTYPE))
    y = jnp.dot(jnp.concatenate(blocks, axis=1), w_ref[...], preferred_element_type=F32)
    xm = x_ref[...] + g1 * y
    xo_ref[...] = xm
    h2_ref[...] = (_rms(xm, n2g_ref[...]) * (1.0 + sc2) + sh2).astype(h2_ref.dtype)


def _merge(gb, z, ysgu, yna, xa, mod, conv_w, out_g, w_out, n2g, tiles_per_batch, seq, n_latent):
    n, d = xa.shape
    tm = TOKEN_TILE
    hb = tm // SUBLANES
    nb8 = n // SUBLANES
    row = lambda w: pl.BlockSpec((tm, w), lambda t: (t, 0))
    full = lambda a: pl.BlockSpec(a.shape, lambda t: (0,) * a.ndim)
    halo_prev = pl.BlockSpec((SUBLANES, CONV_W), lambda t: (jnp.maximum(t * hb - 1, 0), 0))
    halo_next = pl.BlockSpec((SUBLANES, CONV_W), lambda t: (jnp.minimum((t + 1) * hb, nb8 - 1), 0))
    return pl.pallas_call(
        functools.partial(_merge_kernel, tiles_per_batch=tiles_per_batch, seq=seq, n_latent=n_latent),
        grid=(n // tm,),
        in_specs=[row(CONV_W), row(CONV_W), halo_prev, halo_next, row(SGU_W), row(NA_W), row(d),
                  full(mod), full(conv_w), full(out_g), full(w_out), full(n2g)],
        out_specs=[row(d), row(d)],
        out_shape=[jax.ShapeDtypeStruct((n, d), F32), jax.ShapeDtypeStruct((n, d), MXU_DTYPE)],
        compiler_params=_cparams(("parallel",)),
    )(gb, z, z, z, ysgu, yna, xa, mod, conv_w, out_g, w_out, n2g)


def _sort_desc(vals):
    n = len(vals)
    size = 2
    while size <= n:
        k = size // 2
        while k >= 1:
            for i in range(n):
                p = i ^ k
                if p > i:
                    hi, lo = jnp.maximum(vals[i], vals[p]), jnp.minimum(vals[i], vals[p])
                    vals[i], vals[p] = (hi, lo) if (i & size) == 0 else (lo, hi)
            k //= 2
        size *= 2
    return vals


def _merge_bitonic_desc(vals):
    n = len(vals)
    k = n // 2
    while k >= 1:
        for i in range(n):
            if (i & k) == 0:
                hi, lo = jnp.maximum(vals[i], vals[i + k]), jnp.minimum(vals[i], vals[i + k])
                vals[i], vals[i + k] = hi, lo
        k //= 2
    return vals


def _top_merge(a, b):
    n = len(a)
    return _merge_bitonic_desc([jnp.maximum(a[i], b[n - 1 - i]) for i in range(n)])


def _top16_over_keys(tiles):
    vals = _sort_desc(list(tiles))
    for shift in (4, 2, 1):
        vals = _top_merge(vals, [pltpu.roll(v, shift, 0) for v in vals])
    return vals


def _pair_threshold(a, b):
    k = PEER_TOPK
    rows = [[a[i] + b[j] for j in range(k // (i + 1))] for i in range(k)]
    neg = jnp.full(a[0].shape, -jnp.inf, F32)
    pad = lambda vals: vals + [neg] * (k - len(vals))
    l1 = _sort_desc(pad(rows[1] + rows[2]))
    l2 = _sort_desc(pad(rows[3] + rows[4] + rows[5] + rows[6] + rows[7]))
    l3 = _sort_desc(pad([v for r in rows[8:] for v in r]))
    return _top_merge(_top_merge(rows[0], l1), _top_merge(l2, l3))


def _count(pred, acc):
    return acc + jnp.where(pred, 1.0, 0.0)


def _peer_kernel(zero_ref, h2_ref, x_ref, mod_ref, wq_ref, sk_ref, u_ref, vt_ref, o_ref,
                 s_sc, cnt_sc, e0_sc, rk_sc, e1_sc, cnt_st, e0_st, h2t_sc, qt_sc, act_a, act_b, w_a, w_b,
                 acc_sc, *, tiles_per_batch):
    d = D_MODEL
    tm = TOKEN_TILE
    nk = PEER_NKEYS
    key_tiles = nk // SUBLANES
    lane_tiles = tm // LANES
    packed = 2 * SUBLANES
    gate_tiles = nk // packed
    n_pairs = PEER_EXPERT_BLOCK // PEER_PAIR
    e = pl.program_id(1)
    keys_per_step = PEER_EXPERT_BLOCK // nk
    gdt = rk_sc.dtype

    @pl.when(e == 0)
    def _scores_and_thresholds():
        h2t_sc[...] = h2_ref[...].astype(F32).T.astype(h2t_sc.dtype)
        q_t = jnp.dot(wq_ref[...], h2t_sc[...], preferred_element_type=F32)
        qt_sc[...] = q_t.astype(qt_sc.dtype)
        for hp in range(2 * PEER_HEADS):
            s_t = jnp.dot(sk_ref[hp], qt_sc[hp * nk:(hp + 1) * nk, :], preferred_element_type=F32)
            for lt in range(lane_tiles):
                s_sc[hp, lt] = s_t[:, lt * LANES:(lt + 1) * LANES]
        acc_sc[...] = jnp.zeros_like(acc_sc)

        def head_tile(it, carry):
            h = it // lane_tiles
            lt = it % lane_tiles
            rows = lambda v: slice(v * SUBLANES, (v + 1) * SUBLANES)
            s0 = [s_sc[2 * h, lt, rows(v), :] for v in range(key_tiles)]
            s1 = [s_sc[2 * h + 1, lt, rows(v), :] for v in range(key_tiles)]
            a = _top16_over_keys(s0)
            b = _top16_over_keys(s1)
            best = _pair_threshold(a, b)
            tau = best[PEER_TOPK - 1]
            z = jnp.ones_like(tau)
            for kk in range(1, PEER_TOPK):
                z = z + jnp.exp(best[kk] - best[0])
            rz = 1.0 / z
            zero = jnp.zeros_like(tau)
            for v in range(key_tiles):
                cnt = zero
                for kk in range(PEER_TOPK):
                    cnt = _count(s0[v] + b[kk] >= tau, cnt)
                cnt_sc[h, lt, rows(v), :] = cnt
                e0_sc[h, lt, rows(v), :] = jnp.exp(s0[v] - a[0])
            for vp in range(gate_tiles):
                rk, e1 = [], []
                for v in (2 * vp, 2 * vp + 1):
                    r = zero
                    for kk in range(PEER_TOPK):
                        r = _count(b[kk] > s1[v], r)
                    rk.append(r)
                    e1.append(jnp.exp(s1[v] - b[0]) * rz)
                prow = slice(vp * packed, (vp + 1) * packed)
                rk_sc[h, lt, prow, :] = jnp.concatenate(rk, axis=0).astype(gdt)
                e1_sc[h, lt, prow, :] = jnp.concatenate(e1, axis=0).astype(gdt)
            return carry

        lax.fori_loop(0, PEER_HEADS * lane_tiles, head_tile, 0)

    base = pl.multiple_of(e * keys_per_step, SUBLANES)
    for h in range(PEER_HEADS):
        for lt in range(lane_tiles):
            cnt_st[h, lt] = cnt_sc[h, lt, pl.ds(base, SUBLANES), :]
            e0_st[h, lt] = e0_sc[h, lt, pl.ds(base, SUBLANES), :]

    z0 = zero_ref[0]
    act_bufs = (act_a, act_b)
    w_bufs = (w_a, w_b)

    def pre_activation(ip):
        act_bufs[ip % 2][0] = jnp.dot(u_ref[ip], h2t_sc[...], preferred_element_type=F32)

    def accumulate(ip):
        acc_sc[...] += jnp.dot(vt_ref[ip], w_bufs[ip % 2][z0], preferred_element_type=F32)

    def gated_activation(ip):
        kpp = PEER_KEYS_PER_PASS
        for lt in range(lane_tiles):
            lanes = slice(lt * LANES, (lt + 1) * LANES)
            for i0 in range(0, 2, kpp):
                gate = [[jnp.zeros((packed, LANES), gdt) for _ in range(gate_tiles)] for _ in range(kpp)]
                for h in range(PEER_HEADS):
                    row = lambda ref, ii: jnp.broadcast_to(
                        ref[h, lt, 2 * ip + i0 + ii:2 * ip + i0 + ii + 1, :], (packed, LANES)).astype(gdt)
                    cnt = [row(cnt_st, ii) for ii in range(kpp)]
                    e0 = [row(e0_st, ii) for ii in range(kpp)]
                    for jp in range(gate_tiles):
                        prow = slice(jp * packed, (jp + 1) * packed)
                        rk = rk_sc[h, lt, prow, :]
                        e1 = e1_sc[h, lt, prow, :]
                        for ii in range(kpp):
                            gate[ii][jp] = gate[ii][jp] + jnp.where(rk < cnt[ii], e1 * e0[ii],
                                                                     jnp.zeros_like(e1))
                for ii in range(kpp):
                    for jp in range(gate_tiles):
                        r0 = (i0 + ii) * nk + jp * packed
                        rows = slice(r0, r0 + packed)
                        act = jax.nn.gelu(act_bufs[ip % 2][z0, rows, lanes])
                        w_bufs[ip % 2][0, rows, lanes] = (gate[ii][jp] * act.astype(gdt)).astype(w_a.dtype)

    pre_activation(0)
    for ip in range(n_pairs):
        if ip + 1 < n_pairs:
            pre_activation(ip + 1)
        gated_activation(ip)
        if ip >= 1:
            accumulate(ip - 1)
    accumulate(n_pairs - 1)

    @pl.when(e == pl.num_programs(1) - 1)
    def _residual():
        g2 = _mod_row(mod_ref, tiles_per_batch)[:, 5 * d:6 * d]
        o_ref[...] = x_ref[...] + g2 * acc_sc[...].T


def _peer(h2, xm, mod, wq_t, subkeys, u_tab, vt_tab, tiles_per_batch):
    n, d = xm.shape
    tm = TOKEN_TILE
    pairs = PEER_EXPERT_BLOCK // PEER_PAIR
    full = lambda a: pl.BlockSpec(a.shape, lambda t, e: (0,) * a.ndim)
    row = pl.BlockSpec((tm, d), lambda t, e: (t, 0))
    hp2 = 2 * PEER_HEADS
    lt = tm // LANES
    return pl.pallas_call(
        functools.partial(_peer_kernel, tiles_per_batch=tiles_per_batch),
        grid=(n // tm, u_tab.shape[0] // pairs),
        in_specs=[pl.BlockSpec(memory_space=pltpu.SMEM), row, row, full(mod), full(wq_t), full(subkeys),
                  pl.BlockSpec((pairs, PEER_PAIR, d), lambda t, e: (e, 0, 0)),
                  pl.BlockSpec((pairs, d, PEER_PAIR), lambda t, e: (e, 0, 0))],
        out_specs=row,
        out_shape=jax.ShapeDtypeStruct((n, d), F32),
        scratch_shapes=[pltpu.VMEM((hp2, lt, PEER_NKEYS, LANES), F32),
                        pltpu.VMEM((PEER_HEADS, lt, PEER_NKEYS, LANES), F32),
                        pltpu.VMEM((PEER_HEADS, lt, PEER_NKEYS, LANES), F32),
                        pltpu.VMEM((PEER_HEADS, lt, PEER_NKEYS, LANES), GATE_DTYPE),
                        pltpu.VMEM((PEER_HEADS, lt, PEER_NKEYS, LANES), GATE_DTYPE),
                        pltpu.VMEM((PEER_HEADS, lt, SUBLANES, LANES), F32),
                        pltpu.VMEM((PEER_HEADS, lt, SUBLANES, LANES), F32),
                        pltpu.VMEM((d, tm), MXU_DTYPE),
                        pltpu.VMEM((hp2 * PEER_NKEYS, tm), MXU_DTYPE),
                        pltpu.VMEM((1, PEER_PAIR, tm), F32),
                        pltpu.VMEM((1, PEER_PAIR, tm), F32),
                        pltpu.VMEM((1, PEER_PAIR, tm), MXU_DTYPE),
                        pltpu.VMEM((1, PEER_PAIR, tm), MXU_DTYPE),
                        pltpu.VMEM((d, tm), F32)],
        compiler_params=_cparams(("parallel", "arbitrary"), PEER_SCHEDULER_FLAGS),
    )(jnp.zeros((1,), jnp.int32), h2, xm, mod, wq_t, subkeys, u_tab, vt_tab)


def _final_kernel(x_ref, g_ref, o_ref):
    o_ref[...] = _rms(x_ref[...], g_ref[...])


def _final_norm(xa, g, n_latent):
    d = xa.shape[1]
    tm = TOKEN_TILE
    return pl.pallas_call(
        _final_kernel,
        grid=(n_latent // tm,),
        in_specs=[pl.BlockSpec((tm, d), lambda t: (t, 0)), pl.BlockSpec((1, d), lambda t: (0, 0))],
        out_specs=pl.BlockSpec((tm, d), lambda t: (t, 0)),
        out_shape=jax.ShapeDtypeStruct((n_latent, d), F32),
        compiler_params=_cparams(("parallel",)),
    )(xa, g)


def kernel(x, c, ctx, c_ctx, ada_w, ada_b, norm1_g, w_in, conv_w, sgu_w, sgu_b, sgu_norm_g, na_rpb,
           out_norm_g, w_out, norm2_g, peer_wq, peer_subkeys, peer_u, peer_v, final_norm_g):
    batch, seq, d = x.shape
    depth = ada_w.shape[0]
    assert d == D_MODEL and batch == 2 and ctx.shape[1] == CTX_LEN
    assert seq % TOKEN_TILE == 0 and batch * CTX_LEN == TOKEN_TILE
    rows = seq // GRID_W
    assert rows % QUERY_ROWS == 0 and rows >= KEY_ROWS + QUERY_ROWS
    n_latent = batch * seq
    tiles_per_batch = seq // TOKEN_TILE

    xa = jnp.concatenate([x.reshape(n_latent, d), ctx.reshape(batch * CTX_LEN, d)], axis=0)
    cvec = jnp.zeros((SUBLANES, d), F32).at[0:batch].set(c).at[batch].set(c_ctx)
    mods = _adaln(cvec, ada_w, ada_b)
    block_index = _bias_block_index(rows)

    mx = MXU_DTYPE
    w_in_c = w_in.astype(mx)
    w_out_c = w_out.astype(mx)
    sgu_w_c = sgu_w.astype(mx)
    sgu_bias = jnp.repeat(jnp.swapaxes(sgu_b, 1, 2), GROUP_DIM, axis=2)
    wq_t = jnp.swapaxes(peer_wq, 1, 2).astype(mx)
    subkeys = peer_subkeys.reshape(depth, 2 * PEER_HEADS, PEER_NKEYS, PEER_DKEY // 2).astype(mx)
    n_pairs = peer_u.shape[1] // PEER_PAIR
    u_c = peer_u.astype(mx).reshape(depth, n_pairs, PEER_PAIR, d)
    vt_c = jnp.swapaxes(peer_v.astype(mx).reshape(depth, n_pairs, PEER_PAIR, d), 2, 3)

    for l in range(depth):
        mod = mods[l]
        gb, z, ysgu, qkv = _proj(xa, mod, norm1_g[l][None], w_in_c[l], sgu_w_c[l], sgu_bias[l],
                                 sgu_norm_g[l][None], tiles_per_batch)
        yna = _attention(qkv, _build_bias(na_rpb[l], block_index), batch, seq)
        xm, h2 = _merge(gb, z, ysgu, yna, xa, mod, conv_w[l], out_norm_g[l][None], w_out_c[l],
                        norm2_g[l][None], tiles_per_batch, seq, n_latent)
        xa = _peer(h2, xm, mod, wq_t[l], subkeys[l], u_c[l], vt_c[l], tiles_per_batch)
    return _final_norm(xa, final_norm_g[None], n_latent).reshape(batch, seq, d)
```

```python
import functools

import numpy as np
import jax
import jax.numpy as jnp
from jax import lax
from jax.experimental import pallas as pl
from jax.experimental.pallas import tpu as pltpu

D_MODEL = 1024
GRID_W = 64
CTX_LEN = 256
EPS = 1e-6
GROUP_DIM = 64
CONV_W = 256
SGU_W = 256
SGU_GROUPS = 4
CHUNK = 128
NA_HEADS = 8
NA_W = 512
NA_KH = 8
NA_KW = 16
MIX_COLS = 3 * CONV_W + 2 * SGU_W
IN_W = MIX_COLS + 3 * NA_W
PEER_HEADS = 8
PEER_NKEYS = 128
PEER_TOPK = 16
PEER_DKEY = 256

LANES = 128
SUBLANES = 8
TOKEN_TILE = 512
QUERY_ROWS = 4
QUERY_BLOCK = QUERY_ROWS * GRID_W
KEY_ROWS = 12
KEY_BLOCK = KEY_ROWS * GRID_W
MASKED = -1e30
PEER_EXPERT_BLOCK = 1024
PEER_PAIR = 2 * PEER_NKEYS
PEER_KEYS_PER_PASS = 2
VMEM_LIMIT = 56 * 1024 * 1024
PEER_SCHEDULER_FLAGS = None

MXU_DTYPE = jnp.bfloat16
GATE_DTYPE = jnp.bfloat16
F32 = jnp.float32
NT_DIMS = (((1,), (1,)), ((), ()))


def _cparams(sem, flags=None):
    return pltpu.CompilerParams(dimension_semantics=sem, vmem_limit_bytes=VMEM_LIMIT, flags=flags)


def _rms(x, g):
    ms = jnp.mean(x * x, axis=-1, keepdims=True)
    return x * lax.rsqrt(ms + EPS) * g


def _group_rms128(y, g, lo):
    y2 = y * y
    s_lo = jnp.sum(jnp.where(lo, y2, 0.0), axis=-1, keepdims=True)
    s_hi = jnp.sum(jnp.where(lo, 0.0, y2), axis=-1, keepdims=True)
    inv = 1.0 / GROUP_DIM
    r = jnp.where(lo, lax.rsqrt(s_lo * inv + EPS), lax.rsqrt(s_hi * inv + EPS))
    return y * r * g


def _lo_mask():
    return lax.broadcasted_iota(jnp.int32, (1, LANES), 1) < GROUP_DIM


def _mod_row(mod_ref, tiles_per_batch):
    g = jnp.minimum(pl.program_id(0) // tiles_per_batch, 2)
    return mod_ref[pl.ds(g, 1), :]


def _adaln_kernel(c_ref, w_ref, b_ref, o_ref):
    cv = c_ref[...]
    s = cv * (1.0 / (1.0 + jnp.exp(-cv)))
    o_ref[...] = jnp.dot(s.astype(MXU_DTYPE), w_ref[...].astype(MXU_DTYPE),
                         preferred_element_type=F32) + b_ref[...]


def _adaln(cvec, ada_w, ada_b):
    depth, d, cols = ada_w.shape
    cb = 1536
    return pl.pallas_call(
        _adaln_kernel,
        grid=(depth, cols // cb),
        in_specs=[pl.BlockSpec((SUBLANES, d), lambda l, j: (0, 0)),
                  pl.BlockSpec((None, d, cb), lambda l, j: (l, 0, j)),
                  pl.BlockSpec((None, 1, cb), lambda l, j: (l, 0, j))],
        out_specs=pl.BlockSpec((None, SUBLANES, cb), lambda l, j: (l, 0, j)),
        out_shape=jax.ShapeDtypeStruct((depth, SUBLANES, cols), F32),
        compiler_params=_cparams(("parallel", "parallel")),
    )(cvec, ada_w, ada_b.reshape(depth, 1, cols))


def _proj_kernel(x_ref, mod_ref, n1g_ref, w_ref, sw_ref, sb_ref, sg_ref,
                 gb_ref, z_ref, ysgu_ref, qkv_ref, *, tiles_per_batch):
    d = D_MODEL
    mrow = _mod_row(mod_ref, tiles_per_batch)
    sh1, sc1 = mrow[:, 0:d], mrow[:, d:2 * d]
    h = _rms(x_ref[...], n1g_ref[...]) * (1.0 + sc1) + sh1
    p = jnp.dot(h.astype(MXU_DTYPE), w_ref[...], preferred_element_type=F32)
    gb_ref[...] = p[:, 0:CONV_W]
    z_ref[...] = p[:, CONV_W:2 * CONV_W] * p[:, 2 * CONV_W:3 * CONV_W]
    sg = jax.nn.gelu(p[:, 3 * CONV_W:MIX_COLS])
    u, v = sg[:, :SGU_W], sg[:, SGU_W:]
    lo = _lo_mask()
    vn = jnp.concatenate(
        [_group_rms128(v[:, :LANES], sg_ref[:, :LANES], lo),
         _group_rms128(v[:, LANES:], sg_ref[:, LANES:], lo)], axis=1).astype(MXU_DTYPE)
    grp = lax.broadcasted_iota(jnp.int32, (1, SGU_W), 1) // GROUP_DIM
    for c in range(TOKEN_TILE // CHUNK):
        rows = slice(c * CHUNK, (c + 1) * CHUNK)
        vc = vn[rows, :]
        mixed = jnp.zeros((CHUNK, SGU_W), F32)
        for gi in range(SGU_GROUPS):
            mg = jnp.dot(sw_ref[gi], vc, preferred_element_type=F32)
            mixed = jnp.where(grp == gi, mg, mixed)
        ysgu_ref[rows, :] = u[rows, :] * (mixed + sb_ref[...])
    qkv_ref[...] = p[:, MIX_COLS:].astype(qkv_ref.dtype)


def _proj(xa, mod, n1g, w_in, sgu_w, sgu_bias, sgu_g, tiles_per_batch):
    n, d = xa.shape
    tm = TOKEN_TILE
    row = lambda w: pl.BlockSpec((tm, w), lambda t: (t, 0))
    full = lambda a: pl.BlockSpec(a.shape, lambda t: (0,) * a.ndim)
    return pl.pallas_call(
        functools.partial(_proj_kernel, tiles_per_batch=tiles_per_batch),
        grid=(n // tm,),
        in_specs=[row(d), full(mod), full(n1g), full(w_in), full(sgu_w), full(sgu_bias), full(sgu_g)],
        out_specs=[row(CONV_W), row(CONV_W), row(SGU_W), row(3 * NA_W)],
        out_shape=[jax.ShapeDtypeStruct((n, CONV_W), F32), jax.ShapeDtypeStruct((n, CONV_W), F32),
                   jax.ShapeDtypeStruct((n, SGU_W), F32), jax.ShapeDtypeStruct((n, 3 * NA_W), MXU_DTYPE)],
        compiler_params=_cparams(("parallel",)),
    )(xa, mod, n1g, w_in, sgu_w, sgu_bias, sgu_g)


def _attn_kernel(q_ref, k_ref, v_ref, kc_ref, vc_ref, bias_ref, o_ref, *, rows):
    j = pl.program_id(2)
    kr0 = jnp.clip(QUERY_ROWS * j - NA_KH // 2, 0, rows - KEY_ROWS)
    start = pl.multiple_of(kr0 * GRID_W, GRID_W)
    kl = k_ref[pl.ds(start, KEY_BLOCK), :]
    vl = v_ref[pl.ds(start, KEY_BLOCK), :]
    kc, vc, q = kc_ref[...], vc_ref[...], q_ref[...]
    lo = _lo_mask()
    scale = GROUP_DIM ** -0.5
    o = jnp.zeros((QUERY_BLOCK, LANES), F32)
    for hh in range(2):
        hm = lo if hh == 0 else jnp.logical_not(lo)
        qh = jnp.where(hm, q, jnp.zeros_like(q)) * scale
        s_loc = lax.dot_general(qh, kl, NT_DIMS, preferred_element_type=F32) + bias_ref[0, hh]
        s_ctx = lax.dot_general(qh, kc, NT_DIMS, preferred_element_type=F32)
        m = jnp.maximum(jnp.max(s_loc, axis=-1, keepdims=True), jnp.max(s_ctx, axis=-1, keepdims=True))
        p_loc = jnp.exp(s_loc - m)
        p_ctx = jnp.exp(s_ctx - m)
        den = jnp.sum(p_loc, axis=-1, keepdims=True) + jnp.sum(p_ctx, axis=-1, keepdims=True)
        vlh = jnp.where(hm, vl, jnp.zeros_like(vl))
        vch = jnp.where(hm, vc, jnp.zeros_like(vc))
        oh = (jnp.dot(p_loc.astype(MXU_DTYPE), vlh, preferred_element_type=F32)
              + jnp.dot(p_ctx.astype(MXU_DTYPE), vch, preferred_element_type=F32))
        o = o + oh * (1.0 / den)
    o_ref[...] = o


def _attention(qkv, bias, batch, seq):
    n = qkv.shape[0]
    rows = seq // GRID_W
    nblk = rows // QUERY_ROWS
    ctx_q0 = batch * seq // QUERY_BLOCK
    heads_cols = NA_W // LANES

    def q_map(b, hp, j):
        return (jnp.where(j < nblk, b * nblk + j, ctx_q0 + b), hp)

    def variant(b, hp, j):
        v = jnp.where(j == 0, 0, jnp.where(j < nblk - 1, 1, jnp.where(j == nblk - 1, 2, 3)))
        return (v, hp, 0, 0)

    return pl.pallas_call(
        functools.partial(_attn_kernel, rows=rows),
        grid=(batch, heads_cols, nblk + 1),
        in_specs=[pl.BlockSpec((QUERY_BLOCK, LANES), q_map),
                  pl.BlockSpec((seq, LANES), lambda b, hp, j: (b, heads_cols + hp)),
                  pl.BlockSpec((seq, LANES), lambda b, hp, j: (b, 2 * heads_cols + hp)),
                  pl.BlockSpec((CTX_LEN, LANES), lambda b, hp, j: (ctx_q0 + b, heads_cols + hp)),
                  pl.BlockSpec((CTX_LEN, LANES), lambda b, hp, j: (ctx_q0 + b, 2 * heads_cols + hp)),
                  pl.BlockSpec((1, 2, QUERY_BLOCK, KEY_BLOCK), variant)],
        out_specs=pl.BlockSpec((QUERY_BLOCK, LANES), q_map),
        out_shape=jax.ShapeDtypeStruct((n, NA_W), F32),
        compiler_params=_cparams(("parallel", "parallel", "arbitrary")),
    )(qkv, qkv, qkv, qkv, qkv, bias)


def _bias_block_index(rows):
    kh = min(NA_KH, rows)
    idx = np.full((4, QUERY_ROWS, KEY_ROWS), 2 * NA_KH - 1, np.int32)
    for v, (r0, kr0) in enumerate(((0, 0), (QUERY_ROWS, 0), (rows - QUERY_ROWS, rows - KEY_ROWS))):
        for qr in range(QUERY_ROWS):
            r = r0 + qr
            rs = min(max(r - kh // 2, 0), rows - kh)
            for kr in range(KEY_ROWS):
                if rs <= kr0 + kr < rs + kh:
                    idx[v, qr, kr] = kr0 + kr - r + NA_KH - 1
    return idx


def _build_bias(rpb, block_index):
    heads = rpb.shape[0]
    pad = GRID_W - NA_KW
    padded = jnp.pad(rpb, ((0, 0), (0, 0), (pad, pad)), constant_values=MASKED)
    toe = jnp.stack([padded[:, :, GRID_W - 1 - qc:2 * GRID_W - 1 - qc] for qc in range(GRID_W)], axis=2)
    qc = np.arange(GRID_W)[:, None]
    kc = np.arange(GRID_W)[None, :]
    cs = np.clip(qc - NA_KW // 2, 0, GRID_W - NA_KW)
    toe = jnp.where((kc >= cs) & (kc < cs + NA_KW), toe, MASKED)
    toe = jnp.concatenate([toe, jnp.full((heads, 1, GRID_W, GRID_W), MASKED, toe.dtype)], axis=1)
    flat = [int(a) for a in block_index.reshape(-1)]
    blocks = jnp.stack([toe[:, a] for a in flat], axis=1)
    blocks = blocks.reshape(heads, 4, QUERY_ROWS, KEY_ROWS, GRID_W, GRID_W)
    return jnp.transpose(blocks, (1, 0, 2, 4, 3, 5)).reshape(4, heads, QUERY_BLOCK, KEY_BLOCK)


def _merge_kernel(gb_ref, z_ref, zp_ref, zn_ref, ysgu_ref, yna_ref, x_ref, mod_ref, cw_ref, og_ref,
                  w_ref, n2g_ref, xo_ref, h2_ref, *, tiles_per_batch, seq, n_latent):
    d = D_MODEL
    tm = TOKEN_TILE
    t = pl.program_id(0)
    mrow = _mod_row(mod_ref, tiles_per_batch)
    g1, sh2, sc2 = mrow[:, 2 * d:3 * d], mrow[:, 3 * d:4 * d], mrow[:, 4 * d:5 * d]
    z = z_ref[...]
    ridx = lax.broadcasted_iota(jnp.int32, (tm, 1), 0)
    grow = t * tm + ridx
    seqlen = jnp.where(t * tm >= n_latent, CTX_LEN, seq)
    pos = grow & (seqlen - 1)
    z_prev = jnp.where(ridx == 0, zp_ref[SUBLANES - 1:SUBLANES, :], pltpu.roll(z, 1, 0))
    z_next = jnp.where(ridx == tm - 1, zn_ref[0:1, :], pltpu.roll(z, tm - 1, 0))
    z_prev = jnp.where(pos == 0, 0.0, z_prev)
    z_next = jnp.where(pos == seqlen - 1, 0.0, z_next)
    conv = z_prev * cw_ref[0:1, :] + z * cw_ref[1:2, :] + z_next * cw_ref[2:3, :]
    y_conv = gb_ref[...] * conv
    lo = _lo_mask()
    blocks = []
    for src, width in ((y_conv, CONV_W), (ysgu_ref[...], SGU_W), (yna_ref[...], NA_W)):
        for c in range(width // LANES):
            off = len(blocks) * LANES
            blocks.append(_group_rms128(src[:, c * LANES:(c + 1) * LANES], og_ref[:, off:off + LANES], lo)
                          .astype(MXU_DTYPE))
    y = jnp.dot(jnp.concatenate(blocks, axis=1), w_ref[...], preferred_element_type=F32)
    xm = x_ref[...] + g1 * y
    xo_ref[...] = xm
    h2_ref[...] = (_rms(xm, n2g_ref[...]) * (1.0 + sc2) + sh2).astype(h2_ref.dtype)


def _merge(gb, z, ysgu, yna, xa, mod, conv_w, out_g, w_out, n2g, tiles_per_batch, seq, n_latent):
    n, d = xa.shape
    tm = TOKEN_TILE
    hb = tm // SUBLANES
    nb8 = n // SUBLANES
    row = lambda w: pl.BlockSpec((tm, w), lambda t: (t, 0))
    full = lambda a: pl.BlockSpec(a.shape, lambda t: (0,) * a.ndim)
    halo_prev = pl.BlockSpec((SUBLANES, CONV_W), lambda t: (jnp.maximum(t * hb - 1, 0), 0))
    halo_next = pl.BlockSpec((SUBLANES, CONV_W), lambda t: (jnp.minimum((t + 1) * hb, nb8 - 1), 0))
    return pl.pallas_call(
        functools.partial(_merge_kernel, tiles_per_batch=tiles_per_batch, seq=seq, n_latent=n_latent),
        grid=(n // tm,),
        in_specs=[row(CONV_W), row(CONV_W), halo_prev, halo_next, row(SGU_W), row(NA_W), row(d),
                  full(mod), full(conv_w), full(out_g), full(w_out), full(n2g)],
        out_specs=[row(d), row(d)],
        out_shape=[jax.ShapeDtypeStruct((n, d), F32), jax.ShapeDtypeStruct((n, d), MXU_DTYPE)],
        compiler_params=_cparams(("parallel",)),
    )(gb, z, z, z, ysgu, yna, xa, mod, conv_w, out_g, w_out, n2g)


def _sort_desc(vals):
    n = len(vals)
    size = 2
    while size <= n:
        k = size // 2
        while k >= 1:
            for i in range(n):
                p = i ^ k
                if p > i:
                    hi, lo = jnp.maximum(vals[i], vals[p]), jnp.minimum(vals[i], vals[p])
                    vals[i], vals[p] = (hi, lo) if (i & size) == 0 else (lo, hi)
            k //= 2
        size *= 2
    return vals


def _merge_bitonic_desc(vals):
    n = len(vals)
    k = n // 2
    while k >= 1:
        for i in range(n):
            if (i & k) == 0:
                hi, lo = jnp.maximum(vals[i], vals[i + k]), jnp.minimum(vals[i], vals[i + k])
                vals[i], vals[i + k] = hi, lo
        k //= 2
    return vals


def _top_merge(a, b):
    n = len(a)
    return _merge_bitonic_desc([jnp.maximum(a[i], b[n - 1 - i]) for i in range(n)])


def _top16_over_keys(tiles):
    vals = _sort_desc(list(tiles))
    for shift in (4, 2, 1):
        vals = _top_merge(vals, [pltpu.roll(v, shift, 0) for v in vals])
    return vals


def _pair_threshold(a, b):
    k = PEER_TOPK
    rows = [[a[i] + b[j] for j in range(k // (i + 1))] for i in range(k)]
    neg = jnp.full(a[0].shape, -jnp.inf, F32)
    pad = lambda vals: vals + [neg] * (k - len(vals))
    l1 = _sort_desc(pad(rows[1] + rows[2]))
    l2 = _sort_desc(pad(rows[3] + rows[4] + rows[5] + rows[6] + rows[7]))
    l3 = _sort_desc(pad([v for r in rows[8:] for v in r]))
    return _top_merge(_top_merge(rows[0], l1), _top_merge(l2, l3))


def _count(pred, acc):
    return acc + jnp.where(pred, 1.0, 0.0)


def _gelu_tanh(x):
    k1 = -2.0 * (2.0 / np.pi) ** 0.5
    k2 = k1 * 0.044715
    return x / (1.0 + jnp.exp(x * (k1 + k2 * (x * x))))


def _peer_kernel(zero_ref, h2_ref, x_ref, mod_ref, wq_ref, sk_ref, u_ref, vt_ref, o_ref,
                 s_sc, cnt_sc, e0_sc, rk_sc, e1_sc, cnt_st, e0_st, h2t_sc, qt_sc, act_p0, act_p1, act_p2, act_p3,
                 w_sc, acc_sc, *, tiles_per_batch):
    d = D_MODEL
    tm = TOKEN_TILE
    nk = PEER_NKEYS
    key_tiles = nk // SUBLANES
    lane_tiles = tm // LANES
    packed = 2 * SUBLANES
    gate_tiles = nk // packed
    n_pairs = PEER_EXPERT_BLOCK // PEER_PAIR
    n_blocks = nk * nk // PEER_EXPERT_BLOCK
    e = pl.program_id(1)
    gdt = rk_sc.dtype
    assert PEER_EXPERT_BLOCK == SUBLANES * nk

    @pl.when(e == 0)
    def _scores_and_thresholds():
        h2t_sc[...] = h2_ref[...].astype(F32).T.astype(h2t_sc.dtype)
        q_t = jnp.dot(wq_ref[...], h2t_sc[...], preferred_element_type=F32)
        qt_sc[...] = q_t.astype(qt_sc.dtype)
        for hp in range(2 * PEER_HEADS):
            s_t = jnp.dot(sk_ref[hp], qt_sc[hp * nk:(hp + 1) * nk, :], preferred_element_type=F32)
            for lt in range(lane_tiles):
                s_sc[hp, lt] = s_t[:, lt * LANES:(lt + 1) * LANES]
        acc_sc[...] = jnp.zeros_like(acc_sc)

        def head_tile(it, carry):
            h = it // lane_tiles
            lt = it % lane_tiles
            rows = lambda v: slice(v * SUBLANES, (v + 1) * SUBLANES)
            s0 = [s_sc[2 * h, lt, rows(v), :] for v in range(key_tiles)]
            s1 = [s_sc[2 * h + 1, lt, rows(v), :] for v in range(key_tiles)]
            a = _top16_over_keys(s0)
            b = _top16_over_keys(s1)
            best = _pair_threshold(a, b)
            tau = best[PEER_TOPK - 1]
            z = jnp.ones_like(tau)
            for kk in range(1, PEER_TOPK):
                z = z + jnp.exp(best[kk] - best[0])
            rz = 1.0 / z
            zero = jnp.zeros_like(tau)
            for v in range(key_tiles):
                cnt = zero
                for kk in range(PEER_TOPK):
                    cnt = _count(s0[v] + b[kk] >= tau, cnt)
                cnt_sc[h, lt, rows(v), :] = cnt
                e0_sc[h, lt, rows(v), :] = jnp.exp(s0[v] - a[0])
            for vp in range(gate_tiles):
                rk, e1 = [], []
                for v in (2 * vp, 2 * vp + 1):
                    r = zero
                    for kk in range(PEER_TOPK):
                        r = _count(b[kk] > s1[v], r)
                    rk.append(r)
                    e1.append(jnp.exp(s1[v] - b[0]) * rz)
                prow = slice(vp * packed, (vp + 1) * packed)
                rk_sc[h, lt, prow, :] = jnp.concatenate(rk, axis=0).astype(gdt)
                e1_sc[h, lt, prow, :] = jnp.concatenate(e1, axis=0).astype(gdt)
            return carry

        lax.fori_loop(0, PEER_HEADS * lane_tiles, head_tile, 0)

    base = pl.multiple_of(e * SUBLANES, SUBLANES)
    for h in range(PEER_HEADS):
        for lt in range(lane_tiles):
            cnt_st[h, lt] = cnt_sc[h, lt, pl.ds(base, SUBLANES), :]
            e0_st[h, lt] = e0_sc[h, lt, pl.ds(base, SUBLANES), :]
    z0 = zero_ref[0]

    act_bufs = (act_p0, act_p1, act_p2, act_p3)

    def pre_activation(ip):
        act_bufs[ip][0] = jnp.dot(u_ref[ip * PEER_PAIR:(ip + 1) * PEER_PAIR, :], h2t_sc[...],
                                  preferred_element_type=F32)

    def gated_activation(ip):
        kpp = PEER_KEYS_PER_PASS
        for lt in range(lane_tiles):
            lanes = slice(lt * LANES, (lt + 1) * LANES)
            for i0 in range(0, 2, kpp):
                gate = [[jnp.zeros((packed, LANES), gdt) for _ in range(gate_tiles)] for _ in range(kpp)]
                for h in range(PEER_HEADS):
                    row = lambda ref, ii: jnp.broadcast_to(
                        ref[h, lt, 2 * ip + i0 + ii:2 * ip + i0 + ii + 1, :], (packed, LANES)).astype(gdt)
                    cnt = [row(cnt_st, ii) for ii in range(kpp)]
                    e0 = [row(e0_st, ii) for ii in range(kpp)]
                    for jp in range(gate_tiles):
                        prow = slice(jp * packed, (jp + 1) * packed)
                        rk = rk_sc[h, lt, prow, :]
                        e1 = e1_sc[h, lt, prow, :]
                        for ii in range(kpp):
                            gate[ii][jp] = gate[ii][jp] + jnp.where(rk < cnt[ii], e1 * e0[ii],
                                                                     jnp.zeros_like(e1))
                for ii in range(kpp):
                    for jp in range(gate_tiles):
                        r0 = (i0 + ii) * nk + jp * packed
                        act = _gelu_tanh(act_bufs[ip][z0, r0:r0 + packed, lanes])
                        w0 = ip * PEER_PAIR + r0
                        w_sc[0, w0:w0 + packed, lanes] = (gate[ii][jp] * act.astype(gdt)).astype(w_sc.dtype)

    pre_activation(0)
    for ip in range(n_pairs):
        if ip + 1 < n_pairs:
            pre_activation(ip + 1)
        gated_activation(ip)
    acc_sc[...] += jnp.dot(vt_ref[...], w_sc[z0], preferred_element_type=F32)

    @pl.when(e == pl.num_programs(1) - 1)
    def _residual():
        g2 = _mod_row(mod_ref, tiles_per_batch)[:, 5 * d:6 * d]
        o_ref[...] = x_ref[...] + g2 * acc_sc[...].T


def _peer(h2, xm, mod, wq_t, subkeys, u_tab, vt_tab, tiles_per_batch):
    n, d = xm.shape
    tm = TOKEN_TILE
    eb = PEER_EXPERT_BLOCK
    full = lambda a: pl.BlockSpec(a.shape, lambda t, e: (0,) * a.ndim)
    row = pl.BlockSpec((tm, d), lambda t, e: (t, 0))
    hp2 = 2 * PEER_HEADS
    lt = tm // LANES
    return pl.pallas_call(
        functools.partial(_peer_kernel, tiles_per_batch=tiles_per_batch),
        grid=(n // tm, u_tab.shape[0]),
        in_specs=[pl.BlockSpec(memory_space=pltpu.SMEM), row, row, full(mod), full(wq_t), full(subkeys),
                  pl.BlockSpec((None, eb, d), lambda t, e: (e, 0, 0)),
                  pl.BlockSpec((None, d, eb), lambda t, e: (e, 0, 0))],
        out_specs=row,
        out_shape=jax.ShapeDtypeStruct((n, d), F32),
        scratch_shapes=[pltpu.VMEM((hp2, lt, PEER_NKEYS, LANES), F32),
                        pltpu.VMEM((PEER_HEADS, lt, PEER_NKEYS, LANES), F32),
                        pltpu.VMEM((PEER_HEADS, lt, PEER_NKEYS, LANES), F32),
                        pltpu.VMEM((PEER_HEADS, lt, PEER_NKEYS, LANES), GATE_DTYPE),
                        pltpu.VMEM((PEER_HEADS, lt, PEER_NKEYS, LANES), GATE_DTYPE),
                        pltpu.VMEM((PEER_HEADS, lt, SUBLANES, LANES), F32),
                        pltpu.VMEM((PEER_HEADS, lt, SUBLANES, LANES), F32),
                        pltpu.VMEM((d, tm), MXU_DTYPE),
                        pltpu.VMEM((hp2 * PEER_NKEYS, tm), MXU_DTYPE),
                        pltpu.VMEM((1, PEER_PAIR, tm), F32),
                        pltpu.VMEM((1, PEER_PAIR, tm), F32),
                        pltpu.VMEM((1, PEER_PAIR, tm), F32),
                        pltpu.VMEM((1, PEER_PAIR, tm), F32),
                        pltpu.VMEM((1, eb, tm), MXU_DTYPE),
                        pltpu.VMEM((d, tm), F32)],
        compiler_params=_cparams(("parallel", "arbitrary"), PEER_SCHEDULER_FLAGS),
    )(jnp.zeros((1,), jnp.int32), h2, xm, mod, wq_t, subkeys, u_tab, vt_tab)


def _final_kernel(x_ref, g_ref, o_ref):
    o_ref[...] = _rms(x_ref[...], g_ref[...])


def _final_norm(xa, g, n_latent):
    d = xa.shape[1]
    tm = TOKEN_TILE
    return pl.pallas_call(
        _final_kernel,
        grid=(n_latent // tm,),
        in_specs=[pl.BlockSpec((tm, d), lambda t: (t, 0)), pl.BlockSpec((1, d), lambda t: (0, 0))],
        out_specs=pl.BlockSpec((tm, d), lambda t: (t, 0)),
        out_shape=jax.ShapeDtypeStruct((n_latent, d), F32),
        compiler_params=_cparams(("parallel",)),
    )(xa, g)


def kernel(x, c, ctx, c_ctx, ada_w, ada_b, norm1_g, w_in, conv_w, sgu_w, sgu_b, sgu_norm_g, na_rpb,
           out_norm_g, w_out, norm2_g, peer_wq, peer_subkeys, peer_u, peer_v, final_norm_g):
    batch, seq, d = x.shape
    depth = ada_w.shape[0]
    assert d == D_MODEL and batch == 2 and ctx.shape[1] == CTX_LEN
    assert seq % TOKEN_TILE == 0 and batch * CTX_LEN == TOKEN_TILE
    rows = seq // GRID_W
    assert rows % QUERY_ROWS == 0 and rows >= KEY_ROWS + QUERY_ROWS
    n_latent = batch * seq
    tiles_per_batch = seq // TOKEN_TILE

    xa = jnp.concatenate([x.reshape(n_latent, d), ctx.reshape(batch * CTX_LEN, d)], axis=0)
    cvec = jnp.zeros((SUBLANES, d), F32).at[0:batch].set(c).at[batch].set(c_ctx)
    mods = _adaln(cvec, ada_w, ada_b)
    block_index = _bias_block_index(rows)

    mx = MXU_DTYPE
    w_in_c = w_in.astype(mx)
    w_out_c = w_out.astype(mx)
    sgu_w_c = sgu_w.astype(mx)
    sgu_bias = jnp.repeat(jnp.swapaxes(sgu_b, 1, 2), GROUP_DIM, axis=2)
    wq_t = jnp.swapaxes(peer_wq, 1, 2).astype(mx)
    subkeys = peer_subkeys.reshape(depth, 2 * PEER_HEADS, PEER_NKEYS, PEER_DKEY // 2).astype(mx)
    n_blocks = peer_u.shape[1] // PEER_EXPERT_BLOCK
    u_c = peer_u.astype(mx).reshape(depth, n_blocks, PEER_EXPERT_BLOCK, d)
    vt_c = jnp.swapaxes(peer_v.astype(mx).reshape(depth, n_blocks, PEER_EXPERT_BLOCK, d), 2, 3)

    for l in range(depth):
        mod = mods[l]
        gb, z, ysgu, qkv = _proj(xa, mod, norm1_g[l][None], w_in_c[l], sgu_w_c[l], sgu_bias[l],
                                 sgu_norm_g[l][None], tiles_per_batch)
        yna = _attention(qkv, _build_bias(na_rpb[l], block_index), batch, seq)
        xm, h2 = _merge(gb, z, ysgu, yna, xa, mod, conv_w[l], out_norm_g[l][None], w_out_c[l],
                        norm2_g[l][None], tiles_per_batch, seq, n_latent)
        xa = _peer(h2, xm, mod, wq_t[l], subkeys[l], u_c[l], vt_c[l], tiles_per_batch)
    return _final_norm(xa, final_norm_g[None], n_latent).reshape(batch, seq, d)
```

```python
import functools

import numpy as np
import jax
import jax.numpy as jnp
from jax import lax
from jax.experimental import pallas as pl
from jax.experimental.pallas import tpu as pltpu

D_MODEL = 1024
GRID_W = 64
CTX_LEN = 256
EPS = 1e-6
GROUP_DIM = 64
CONV_W = 256
SGU_W = 256
SGU_GROUPS = 4
CHUNK = 128
NA_HEADS = 8
NA_W = 512
NA_KH = 8
NA_KW = 16
MIX_COLS = 3 * CONV_W + 2 * SGU_W
IN_W = MIX_COLS + 3 * NA_W
PEER_HEADS = 8
PEER_NKEYS = 128
PEER_TOPK = 16
PEER_DKEY = 256

LANES = 128
SUBLANES = 8
TOKEN_TILE = 512
QUERY_ROWS = 4
QUERY_BLOCK = QUERY_ROWS * GRID_W
KEY_ROWS = 12
KEY_BLOCK = KEY_ROWS * GRID_W
MASKED = -1e30
PEER_EXPERT_BLOCK = 1024
PEER_PAIR = 2 * PEER_NKEYS
PEER_KEYS_PER_PASS = 2
PEER_OUT_CHUNKS = 4
VMEM_LIMIT = 56 * 1024 * 1024
PEER_SCHEDULER_FLAGS = None

MXU_DTYPE = jnp.bfloat16
GATE_DTYPE = jnp.bfloat16
F32 = jnp.float32
NT_DIMS = (((1,), (1,)), ((), ()))


def _cparams(sem, flags=None):
    return pltpu.CompilerParams(dimension_semantics=sem, vmem_limit_bytes=VMEM_LIMIT, flags=flags)


def _rms(x, g):
    ms = jnp.mean(x * x, axis=-1, keepdims=True)
    return x * lax.rsqrt(ms + EPS) * g


def _group_rms128(y, g, lo):
    y2 = y * y
    s_lo = jnp.sum(jnp.where(lo, y2, 0.0), axis=-1, keepdims=True)
    s_hi = jnp.sum(jnp.where(lo, 0.0, y2), axis=-1, keepdims=True)
    inv = 1.0 / GROUP_DIM
    r = jnp.where(lo, lax.rsqrt(s_lo * inv + EPS), lax.rsqrt(s_hi * inv + EPS))
    return y * r * g


def _lo_mask():
    return lax.broadcasted_iota(jnp.int32, (1, LANES), 1) < GROUP_DIM


def _mod_row(mod_ref, tiles_per_batch):
    g = jnp.minimum(pl.program_id(0) // tiles_per_batch, 2)
    return mod_ref[pl.ds(g, 1), :]


def _adaln_kernel(c_ref, w_ref, b_ref, o_ref):
    cv = c_ref[...]
    s = cv * (1.0 / (1.0 + jnp.exp(-cv)))
    o_ref[...] = jnp.dot(s.astype(MXU_DTYPE), w_ref[...].astype(MXU_DTYPE),
                         preferred_element_type=F32) + b_ref[...]


def _adaln(cvec, ada_w, ada_b):
    depth, d, cols = ada_w.shape
    cb = 1536
    return pl.pallas_call(
        _adaln_kernel,
        grid=(depth, cols // cb),
        in_specs=[pl.BlockSpec((SUBLANES, d), lambda l, j: (0, 0)),
                  pl.BlockSpec((None, d, cb), lambda l, j: (l, 0, j)),
                  pl.BlockSpec((None, 1, cb), lambda l, j: (l, 0, j))],
        out_specs=pl.BlockSpec((None, SUBLANES, cb), lambda l, j: (l, 0, j)),
        out_shape=jax.ShapeDtypeStruct((depth, SUBLANES, cols), F32),
        compiler_params=_cparams(("parallel", "parallel")),
    )(cvec, ada_w, ada_b.reshape(depth, 1, cols))


def _proj_kernel(x_ref, mod_ref, n1g_ref, w_ref, sw_ref, sb_ref, sg_ref,
                 gb_ref, z_ref, ysgu_ref, qkv_ref, *, tiles_per_batch):
    d = D_MODEL
    mrow = _mod_row(mod_ref, tiles_per_batch)
    sh1, sc1 = mrow[:, 0:d], mrow[:, d:2 * d]
    h = _rms(x_ref[...], n1g_ref[...]) * (1.0 + sc1) + sh1
    p = jnp.dot(h.astype(MXU_DTYPE), w_ref[...], preferred_element_type=F32)
    gb_ref[...] = p[:, 0:CONV_W]
    z_ref[...] = p[:, CONV_W:2 * CONV_W] * p[:, 2 * CONV_W:3 * CONV_W]
    sg = jax.nn.gelu(p[:, 3 * CONV_W:MIX_COLS])
    u, v = sg[:, :SGU_W], sg[:, SGU_W:]
    lo = _lo_mask()
    vn = jnp.concatenate(
        [_group_rms128(v[:, :LANES], sg_ref[:, :LANES], lo),
         _group_rms128(v[:, LANES:], sg_ref[:, LANES:], lo)], axis=1).astype(MXU_DTYPE)
    grp = lax.broadcasted_iota(jnp.int32, (1, SGU_W), 1) // GROUP_DIM
    for c in range(TOKEN_TILE // CHUNK):
        rows = slice(c * CHUNK, (c + 1) * CHUNK)
        vc = vn[rows, :]
        mixed = jnp.zeros((CHUNK, SGU_W), F32)
        for gi in range(SGU_GROUPS):
            mg = jnp.dot(sw_ref[gi], vc, preferred_element_type=F32)
            mixed = jnp.where(grp == gi, mg, mixed)
        ysgu_ref[rows, :] = u[rows, :] * (mixed + sb_ref[...])
    qkv_ref[...] = p[:, MIX_COLS:].astype(qkv_ref.dtype)


def _proj(xa, mod, n1g, w_in, sgu_w, sgu_bias, sgu_g, tiles_per_batch):
    n, d = xa.shape
    tm = TOKEN_TILE
    row = lambda w: pl.BlockSpec((tm, w), lambda t: (t, 0))
    full = lambda a: pl.BlockSpec(a.shape, lambda t: (0,) * a.ndim)
    return pl.pallas_call(
        functools.partial(_proj_kernel, tiles_per_batch=tiles_per_batch),
        grid=(n // tm,),
        in_specs=[row(d), full(mod), full(n1g), full(w_in), full(sgu_w), full(sgu_bias), full(sgu_g)],
        out_specs=[row(CONV_W), row(CONV_W), row(SGU_W), row(3 * NA_W)],
        out_shape=[jax.ShapeDtypeStruct((n, CONV_W), F32), jax.ShapeDtypeStruct((n, CONV_W), F32),
                   jax.ShapeDtypeStruct((n, SGU_W), F32), jax.ShapeDtypeStruct((n, 3 * NA_W), MXU_DTYPE)],
        compiler_params=_cparams(("parallel",)),
    )(xa, mod, n1g, w_in, sgu_w, sgu_bias, sgu_g)


def _attn_kernel(q_ref, k_ref, v_ref, kc_ref, vc_ref, bias_ref, o_ref, *, rows):
    j = pl.program_id(2)
    kr0 = jnp.clip(QUERY_ROWS * j - NA_KH // 2, 0, rows - KEY_ROWS)
    start = pl.multiple_of(kr0 * GRID_W, GRID_W)
    kl = k_ref[pl.ds(start, KEY_BLOCK), :]
    vl = v_ref[pl.ds(start, KEY_BLOCK), :]
    kc, vc, q = kc_ref[...], vc_ref[...], q_ref[...]
    lo = _lo_mask()
    scale = GROUP_DIM ** -0.5
    o = jnp.zeros((QUERY_BLOCK, LANES), F32)
    for hh in range(2):
        hm = lo if hh == 0 else jnp.logical_not(lo)
        qh = jnp.where(hm, q, jnp.zeros_like(q)) * scale
        s_loc = lax.dot_general(qh, kl, NT_DIMS, preferred_element_type=F32) + bias_ref[0, hh]
        s_ctx = lax.dot_general(qh, kc, NT_DIMS, preferred_element_type=F32)
        m = jnp.maximum(jnp.max(s_loc, axis=-1, keepdims=True), jnp.max(s_ctx, axis=-1, keepdims=True))
        p_loc = jnp.exp(s_loc - m)
        p_ctx = jnp.exp(s_ctx - m)
        den = jnp.sum(p_loc, axis=-1, keepdims=True) + jnp.sum(p_ctx, axis=-1, keepdims=True)
        vlh = jnp.where(hm, vl, jnp.zeros_like(vl))
        vch = jnp.where(hm, vc, jnp.zeros_like(vc))
        oh = (jnp.dot(p_loc.astype(MXU_DTYPE), vlh, preferred_element_type=F32)
              + jnp.dot(p_ctx.astype(MXU_DTYPE), vch, preferred_element_type=F32))
        o = o + oh * (1.0 / den)
    o_ref[...] = o


def _attention(qkv, bias, batch, seq):
    n = qkv.shape[0]
    rows = seq // GRID_W
    nblk = rows // QUERY_ROWS
    ctx_q0 = batch * seq // QUERY_BLOCK
    heads_cols = NA_W // LANES

    def q_map(b, hp, j):
        return (jnp.where(j < nblk, b * nblk + j, ctx_q0 + b), hp)

    def variant(b, hp, j):
        v = jnp.where(j == 0, 0, jnp.where(j < nblk - 1, 1, jnp.where(j == nblk - 1, 2, 3)))
        return (v, hp, 0, 0)

    return pl.pallas_call(
        functools.partial(_attn_kernel, rows=rows),
        grid=(batch, heads_cols, nblk + 1),
        in_specs=[pl.BlockSpec((QUERY_BLOCK, LANES), q_map),
                  pl.BlockSpec((seq, LANES), lambda b, hp, j: (b, heads_cols + hp)),
                  pl.BlockSpec((seq, LANES), lambda b, hp, j: (b, 2 * heads_cols + hp)),
                  pl.BlockSpec((CTX_LEN, LANES), lambda b, hp, j: (ctx_q0 + b, heads_cols + hp)),
                  pl.BlockSpec((CTX_LEN, LANES), lambda b, hp, j: (ctx_q0 + b, 2 * heads_cols + hp)),
                  pl.BlockSpec((1, 2, QUERY_BLOCK, KEY_BLOCK), variant)],
        out_specs=pl.BlockSpec((QUERY_BLOCK, LANES), q_map),
        out_shape=jax.ShapeDtypeStruct((n, NA_W), F32),
        compiler_params=_cparams(("parallel", "parallel", "arbitrary")),
    )(qkv, qkv, qkv, qkv, qkv, bias)


def _bias_block_index(rows):
    kh = min(NA_KH, rows)
    idx = np.full((4, QUERY_ROWS, KEY_ROWS), 2 * NA_KH - 1, np.int32)
    for v, (r0, kr0) in enumerate(((0, 0), (QUERY_ROWS, 0), (rows - QUERY_ROWS, rows - KEY_ROWS))):
        for qr in range(QUERY_ROWS):
            r = r0 + qr
            rs = min(max(r - kh // 2, 0), rows - kh)
            for kr in range(KEY_ROWS):
                if rs <= kr0 + kr < rs + kh:
                    idx[v, qr, kr] = kr0 + kr - r + NA_KH - 1
    return idx


def _build_bias(rpb, block_index):
    heads = rpb.shape[0]
    pad = GRID_W - NA_KW
    padded = jnp.pad(rpb, ((0, 0), (0, 0), (pad, pad)), constant_values=MASKED)
    toe = jnp.stack([padded[:, :, GRID_W - 1 - qc:2 * GRID_W - 1 - qc] for qc in range(GRID_W)], axis=2)
    qc = np.arange(GRID_W)[:, None]
    kc = np.arange(GRID_W)[None, :]
    cs = np.clip(qc - NA_KW // 2, 0, GRID_W - NA_KW)
    toe = jnp.where((kc >= cs) & (kc < cs + NA_KW), toe, MASKED)
    toe = jnp.concatenate([toe, jnp.full((heads, 1, GRID_W, GRID_W), MASKED, toe.dtype)], axis=1)
    flat = [int(a) for a in block_index.reshape(-1)]
    blocks = jnp.stack([toe[:, a] for a in flat], axis=1)
    blocks = blocks.reshape(heads, 4, QUERY_ROWS, KEY_ROWS, GRID_W, GRID_W)
    return jnp.transpose(blocks, (1, 0, 2, 4, 3, 5)).reshape(4, heads, QUERY_BLOCK, KEY_BLOCK)


def _merge_kernel(gb_ref, z_ref, zp_ref, zn_ref, ysgu_ref, yna_ref, x_ref, mod_ref, cw_ref, og_ref,
                  w_ref, n2g_ref, xo_ref, h2_ref, *, tiles_per_batch, seq, n_latent):
    d = D_MODEL
    tm = TOKEN_TILE
    t = pl.program_id(0)
    mrow = _mod_row(mod_ref, tiles_per_batch)
    g1, sh2, sc2 = mrow[:, 2 * d:3 * d], mrow[:, 3 * d:4 * d], mrow[:, 4 * d:5 * d]
    z = z_ref[...]
    ridx = lax.broadcasted_iota(jnp.int32, (tm, 1), 0)
    grow = t * tm + ridx
    seqlen = jnp.where(t * tm >= n_latent, CTX_LEN, seq)
    pos = grow & (seqlen - 1)
    z_prev = jnp.where(ridx == 0, zp_ref[SUBLANES - 1:SUBLANES, :], pltpu.roll(z, 1, 0))
    z_next = jnp.where(ridx == tm - 1, zn_ref[0:1, :], pltpu.roll(z, tm - 1, 0))
    z_prev = jnp.where(pos == 0, 0.0, z_prev)
    z_next = jnp.where(pos == seqlen - 1, 0.0, z_next)
    conv = z_prev * cw_ref[0:1, :] + z * cw_ref[1:2, :] + z_next * cw_ref[2:3, :]
    y_conv = gb_ref[...] * conv
    lo = _lo_mask()
    blocks = []
    for src, width in ((y_conv, CONV_W), (ysgu_ref[...], SGU_W), (yna_ref[...], NA_W)):
        for c in range(width // LANES):
            off = len(blocks) * LANES
            blocks.append(_group_rms128(src[:, c * LANES:(c + 1) * LANES], og_ref[:, off:off + LANES], lo)
                          .astype(MXU_DTYPE))
    y = jnp.dot(jnp.concatenate(blocks, axis=1), w_ref[...], preferred_element_type=F32)
    xm = x_ref[...] + g1 * y
    xo_ref[...] = xm
    h2_ref[...] = (_rms(xm, n2g_ref[...]) * (1.0 + sc2) + sh2).astype(h2_ref.dtype)


def _merge(gb, z, ysgu, yna, xa, mod, conv_w, out_g, w_out, n2g, tiles_per_batch, seq, n_latent):
    n, d = xa.shape
    tm = TOKEN_TILE
    hb = tm // SUBLANES
    nb8 = n // SUBLANES
    row = lambda w: pl.BlockSpec((tm, w), lambda t: (t, 0))
    full = lambda a: pl.BlockSpec(a.shape, lambda t: (0,) * a.ndim)
    halo_prev = pl.BlockSpec((SUBLANES, CONV_W), lambda t: (jnp.maximum(t * hb - 1, 0), 0))
    halo_next = pl.BlockSpec((SUBLANES, CONV_W), lambda t: (jnp.minimum((t + 1) * hb, nb8 - 1), 0))
    return pl.pallas_call(
        functools.partial(_merge_kernel, tiles_per_batch=tiles_per_batch, seq=seq, n_latent=n_latent),
        grid=(n // tm,),
        in_specs=[row(CONV_W), row(CONV_W), halo_prev, halo_next, row(SGU_W), row(NA_W), row(d),
                  full(mod), full(conv_w), full(out_g), full(w_out), full(n2g)],
        out_specs=[row(d), row(d)],
        out_shape=[jax.ShapeDtypeStruct((n, d), F32), jax.ShapeDtypeStruct((n, d), MXU_DTYPE)],
        compiler_params=_cparams(("parallel",)),
    )(gb, z, z, z, ysgu, yna, xa, mod, conv_w, out_g, w_out, n2g)


def _sort_desc(vals):
    n = len(vals)
    size = 2
    while size <= n:
        k = size // 2
        while k >= 1:
            for i in range(n):
                p = i ^ k
                if p > i:
                    hi, lo = jnp.maximum(vals[i], vals[p]), jnp.minimum(vals[i], vals[p])
                    vals[i], vals[p] = (hi, lo) if (i & size) == 0 else (lo, hi)
            k //= 2
        size *= 2
    return vals


def _merge_bitonic_desc(vals):
    n = len(vals)
    k = n // 2
    while k >= 1:
        for i in range(n):
            if (i & k) == 0:
                hi, lo = jnp.maximum(vals[i], vals[i + k]), jnp.minimum(vals[i], vals[i + k])
                vals[i], vals[i + k] = hi, lo
        k //= 2
    return vals


def _top_merge(a, b):
    n = len(a)
    return _merge_bitonic_desc([jnp.maximum(a[i], b[n - 1 - i]) for i in range(n)])


def _top16_over_keys(tiles):
    vals = _sort_desc(list(tiles))
    for shift in (4, 2, 1):
        vals = _top_merge(vals, [pltpu.roll(v, shift, 0) for v in vals])
    return vals


def _pair_threshold(a, b):
    k = PEER_TOPK
    rows = [[a[i] + b[j] for j in range(k // (i + 1))] for i in range(k)]
    neg = jnp.full(a[0].shape, -jnp.inf, F32)
    pad = lambda vals: vals + [neg] * (k - len(vals))
    l1 = _sort_desc(pad(rows[1] + rows[2]))
    l2 = _sort_desc(pad(rows[3] + rows[4] + rows[5] + rows[6] + rows[7]))
    l3 = _sort_desc(pad([v for r in rows[8:] for v in r]))
    return _top_merge(_top_merge(rows[0], l1), _top_merge(l2, l3))


def _sorted_count(b, pred):
    one = lambda p, v: jnp.where(p, v, 0.0)
    p8 = pred(b[7])
    p4 = pred(jnp.where(p8, b[11], b[3]))
    p2 = pred(jnp.where(p8, jnp.where(p4, b[13], b[9]), jnp.where(p4, b[5], b[1])))
    lo = jnp.where(p4, jnp.where(p2, b[6], b[4]), jnp.where(p2, b[2], b[0]))
    hi = jnp.where(p4, jnp.where(p2, b[14], b[12]), jnp.where(p2, b[10], b[8]))
    p1 = pred(jnp.where(p8, hi, lo))
    p0 = pred(b[15])
    return one(p8, 8.0) + one(p4, 4.0) + one(p2, 2.0) + one(p1, 1.0) + one(p0, 1.0)


def _gelu_tanh(x):
    k1 = -2.0 * (2.0 / np.pi) ** 0.5
    k2 = k1 * 0.044715
    return x / (1.0 + jnp.exp(x * (k1 + k2 * (x * x))))


def _peer_kernel(zero_ref, h2_ref, x_ref, mod_ref, wq_ref, sk_ref, u_ref, vt_ref, o_ref,
                 s_sc, cnt_sc, e0_sc, rk_sc, e1_sc, cnt_st, e0_st, h2t_sc, qt_sc, act_p0, act_p1, act_p2, act_p3,
                 w_sc, acc_sc, *, tiles_per_batch):
    d = D_MODEL
    tm = TOKEN_TILE
    nk = PEER_NKEYS
    key_tiles = nk // SUBLANES
    lane_tiles = tm // LANES
    packed = 2 * SUBLANES
    gate_tiles = nk // packed
    n_pairs = PEER_EXPERT_BLOCK // PEER_PAIR
    n_blocks = nk * nk // PEER_EXPERT_BLOCK
    e = pl.program_id(1)
    gdt = rk_sc.dtype
    assert PEER_EXPERT_BLOCK == SUBLANES * nk

    @pl.when(e == 0)
    def _scores_and_thresholds():
        h2t_sc[...] = h2_ref[...].astype(F32).T.astype(h2t_sc.dtype)
        q_t = jnp.dot(wq_ref[...], h2t_sc[...], preferred_element_type=F32)
        qt_sc[...] = q_t.astype(qt_sc.dtype)
        for hp in range(2 * PEER_HEADS):
            s_t = jnp.dot(sk_ref[hp], qt_sc[hp * nk:(hp + 1) * nk, :], preferred_element_type=F32)
            for lt in range(lane_tiles):
                s_sc[hp, lt] = s_t[:, lt * LANES:(lt + 1) * LANES]
        acc_sc[...] = jnp.zeros_like(acc_sc)

        def head_tile(it, carry):
            h = it // lane_tiles
            lt = it % lane_tiles
            rows = lambda v: slice(v * SUBLANES, (v + 1) * SUBLANES)
            s0 = [s_sc[2 * h, lt, rows(v), :] for v in range(key_tiles)]
            s1 = [s_sc[2 * h + 1, lt, rows(v), :] for v in range(key_tiles)]
            a = _top16_over_keys(s0)
            b = _top16_over_keys(s1)
            best = _pair_threshold(a, b)
            tau = best[PEER_TOPK - 1]
            z = jnp.ones_like(tau)
            for kk in range(1, PEER_TOPK):
                z = z + jnp.exp(best[kk] - best[0])
            rz = 1.0 / z
            for v in range(key_tiles):
                cnt_sc[h, lt, rows(v), :] = _sorted_count(b, lambda bk: s0[v] + bk >= tau)
                e0_sc[h, lt, rows(v), :] = jnp.exp(s0[v] - a[0])
            for vp in range(gate_tiles):
                rk, e1 = [], []
                for v in (2 * vp, 2 * vp + 1):
                    rk.append(_sorted_count(b, lambda bk: bk > s1[v]))
                    e1.append(jnp.exp(s1[v] - b[0]) * rz)
                prow = slice(vp * packed, (vp + 1) * packed)
                rk_sc[h, lt, prow, :] = jnp.concatenate(rk, axis=0).astype(gdt)
                e1_sc[h, lt, prow, :] = jnp.concatenate(e1, axis=0).astype(gdt)
            return carry

        lax.fori_loop(0, PEER_HEADS * lane_tiles, head_tile, 0)

    base = pl.multiple_of(e * SUBLANES, SUBLANES)
    for h in range(PEER_HEADS):
        for lt in range(lane_tiles):
            cnt_st[h, lt] = cnt_sc[h, lt, pl.ds(base, SUBLANES), :]
            e0_st[h, lt] = e0_sc[h, lt, pl.ds(base, SUBLANES), :]
    z0 = zero_ref[0]

    act_bufs = (act_p0, act_p1, act_p2, act_p3)

    def pre_activation(ip):
        act_bufs[ip][0] = jnp.dot(u_ref[ip * PEER_PAIR:(ip + 1) * PEER_PAIR, :], h2t_sc[...],
                                  preferred_element_type=F32)

    def gated_activation(ip):
        kpp = PEER_KEYS_PER_PASS
        for lt in range(lane_tiles):
            lanes = slice(lt * LANES, (lt + 1) * LANES)
            for i0 in range(0, 2, kpp):
                gate = [[jnp.zeros((packed, LANES), gdt) for _ in range(gate_tiles)] for _ in range(kpp)]
                for h in range(PEER_HEADS):
                    row = lambda ref, ii: jnp.broadcast_to(
                        ref[h, lt, 2 * ip + i0 + ii:2 * ip + i0 + ii + 1, :], (packed, LANES)).astype(gdt)
                    cnt = [row(cnt_st, ii) for ii in range(kpp)]
                    e0 = [row(e0_st, ii) for ii in range(kpp)]
                    for jp in range(gate_tiles):
                        prow = slice(jp * packed, (jp + 1) * packed)
                        rk = rk_sc[h, lt, prow, :]
                        e1 = e1_sc[h, lt, prow, :]
                        for ii in range(kpp):
                            gate[ii][jp] = gate[ii][jp] + jnp.where(rk < cnt[ii], e1 * e0[ii],
                                                                     jnp.zeros_like(e1))
                for ii in range(kpp):
                    for jp in range(gate_tiles):
                        r0 = (i0 + ii) * nk + jp * packed
                        act = _gelu_tanh(act_bufs[ip][z0, r0:r0 + packed, lanes].astype(gdt))
                        w0 = ip * PEER_PAIR + r0
                        w_sc[0, w0:w0 + packed, lanes] = (gate[ii][jp] * act).astype(w_sc.dtype)

    pre_activation(0)
    for ip in range(n_pairs):
        if ip + 1 < n_pairs:
            pre_activation(ip + 1)
        gated_activation(ip)
    for ic in range(PEER_OUT_CHUNKS):
        rows = slice(ic * (d // PEER_OUT_CHUNKS), (ic + 1) * (d // PEER_OUT_CHUNKS))
        acc_sc[rows, :] += jnp.dot(vt_ref[rows, :], w_sc[z0], preferred_element_type=F32)

    @pl.when(e == pl.num_programs(1) - 1)
    def _residual():
        g2 = _mod_row(mod_ref, tiles_per_batch)[:, 5 * d:6 * d]
        o_ref[...] = x_ref[...] + g2 * acc_sc[...].T


def _peer(h2, xm, mod, wq_t, subkeys, u_tab, vt_tab, tiles_per_batch):
    n, d = xm.shape
    tm = TOKEN_TILE
    eb = PEER_EXPERT_BLOCK
    nb = u_tab.shape[0]
    full = lambda a: pl.BlockSpec(a.shape, lambda t, e: (0,) * a.ndim)
    row = pl.BlockSpec((tm, d), lambda t, e: (t, 0))
    hp2 = 2 * PEER_HEADS
    lt = tm // LANES
    return pl.pallas_call(
        functools.partial(_peer_kernel, tiles_per_batch=tiles_per_batch),
        grid=(n // tm, nb),
        in_specs=[pl.BlockSpec(memory_space=pltpu.SMEM), row, row, full(mod), full(wq_t), full(subkeys),
                  pl.BlockSpec((None, eb, d), lambda t, e: (e, 0, 0)),
                  pl.BlockSpec((None, d, eb), lambda t, e: (e, 0, 0))],
        out_specs=row,
        out_shape=jax.ShapeDtypeStruct((n, d), F32),
        scratch_shapes=[pltpu.VMEM((hp2, lt, PEER_NKEYS, LANES), F32),
                        pltpu.VMEM((PEER_HEADS, lt, PEER_NKEYS, LANES), F32),
                        pltpu.VMEM((PEER_HEADS, lt, PEER_NKEYS, LANES), F32),
                        pltpu.VMEM((PEER_HEADS, lt, PEER_NKEYS, LANES), GATE_DTYPE),
                        pltpu.VMEM((PEER_HEADS, lt, PEER_NKEYS, LANES), GATE_DTYPE),
                        pltpu.VMEM((PEER_HEADS, lt, SUBLANES, LANES), F32),
                        pltpu.VMEM((PEER_HEADS, lt, SUBLANES, LANES), F32),
                        pltpu.VMEM((d, tm), MXU_DTYPE),
                        pltpu.VMEM((hp2 * PEER_NKEYS, tm), MXU_DTYPE),
                        pltpu.VMEM((1, PEER_PAIR, tm), F32),
                        pltpu.VMEM((1, PEER_PAIR, tm), F32),
                        pltpu.VMEM((1, PEER_PAIR, tm), F32),
                        pltpu.VMEM((1, PEER_PAIR, tm), F32),
                        pltpu.VMEM((1, eb, tm), MXU_DTYPE),
                        pltpu.VMEM((d, tm), F32)],
        compiler_params=_cparams(("parallel", "arbitrary"), PEER_SCHEDULER_FLAGS),
    )(jnp.zeros((1,), jnp.int32), h2, xm, mod, wq_t, subkeys, u_tab, vt_tab)


def _final_kernel(x_ref, g_ref, o_ref):
    o_ref[...] = _rms(x_ref[...], g_ref[...])


def _final_norm(xa, g, n_latent):
    d = xa.shape[1]
    tm = TOKEN_TILE
    return pl.pallas_call(
        _final_kernel,
        grid=(n_latent // tm,),
        in_specs=[pl.BlockSpec((tm, d), lambda t: (t, 0)), pl.BlockSpec((1, d), lambda t: (0, 0))],
        out_specs=pl.BlockSpec((tm, d), lambda t: (t, 0)),
        out_shape=jax.ShapeDtypeStruct((n_latent, d), F32),
        compiler_params=_cparams(("parallel",)),
    )(xa, g)


def kernel(x, c, ctx, c_ctx, ada_w, ada_b, norm1_g, w_in, conv_w, sgu_w, sgu_b, sgu_norm_g, na_rpb,
           out_norm_g, w_out, norm2_g, peer_wq, peer_subkeys, peer_u, peer_v, final_norm_g):
    batch, seq, d = x.shape
    depth = ada_w.shape[0]
    assert d == D_MODEL and batch == 2 and ctx.shape[1] == CTX_LEN
    assert seq % TOKEN_TILE == 0 and batch * CTX_LEN == TOKEN_TILE
    rows = seq // GRID_W
    assert rows % QUERY_ROWS == 0 and rows >= KEY_ROWS + QUERY_ROWS
    n_latent = batch * seq
    tiles_per_batch = seq // TOKEN_TILE

    xa = jnp.concatenate([x.reshape(n_latent, d), ctx.reshape(batch * CTX_LEN, d)], axis=0)
    cvec = jnp.zeros((SUBLANES, d), F32).at[0:batch].set(c).at[batch].set(c_ctx)
    mods = _adaln(cvec, ada_w, ada_b)
    block_index = _bias_block_index(rows)

    mx = MXU_DTYPE
    w_in_c = w_in.astype(mx)
    w_out_c = w_out.astype(mx)
    sgu_w_c = sgu_w.astype(mx)
    sgu_bias = jnp.repeat(jnp.swapaxes(sgu_b, 1, 2), GROUP_DIM, axis=2)
    wq_t = jnp.swapaxes(peer_wq, 1, 2).astype(mx)
    subkeys = peer_subkeys.reshape(depth, 2 * PEER_HEADS, PEER_NKEYS, PEER_DKEY // 2).astype(mx)
    n_blocks = peer_u.shape[1] // PEER_EXPERT_BLOCK
    u_c = peer_u.astype(mx).reshape(depth, n_blocks, PEER_EXPERT_BLOCK, d)
    vt_c = jnp.swapaxes(peer_v.astype(mx).reshape(depth, n_blocks, PEER_EXPERT_BLOCK, d), 2, 3)

    for l in range(depth):
        mod = mods[l]
        gb, z, ysgu, qkv = _proj(xa, mod, norm1_g[l][None], w_in_c[l], sgu_w_c[l], sgu_bias[l],
                                 sgu_norm_g[l][None], tiles_per_batch)
        yna = _attention(qkv, _build_bias(na_rpb[l], block_index), batch, seq)
        xm, h2 = _merge(gb, z, ysgu, yna, xa, mod, conv_w[l], out_norm_g[l][None], w_out_c[l],
                        norm2_g[l][None], tiles_per_batch, seq, n_latent)
        xa = _peer(h2, xm, mod, wq_t[l], subkeys[l], u_c[l], vt_c[l], tiles_per_batch)
    return _final_norm(xa, final_norm_g[None], n_latent).reshape(batch, seq, d)
```

```python
import functools

import numpy as np
import jax
import jax.numpy as jnp
from jax import lax
from jax.experimental import pallas as pl
from jax.experimental.pallas import tpu as pltpu

D_MODEL = 1024
GRID_W = 64
CTX_LEN = 256
EPS = 1e-6
GROUP_DIM = 64
CONV_W = 256
SGU_W = 256
SGU_GROUPS = 4
CHUNK = 128
NA_HEADS = 8
NA_W = 512
NA_KH = 8
NA_KW = 16
MIX_COLS = 3 * CONV_W + 2 * SGU_W
IN_W = MIX_COLS + 3 * NA_W
PEER_HEADS = 8
PEER_NKEYS = 128
PEER_TOPK = 16
PEER_DKEY = 256

LANES = 128
SUBLANES = 8
TOKEN_TILE = 512
QUERY_ROWS = 4
QUERY_BLOCK = QUERY_ROWS * GRID_W
KEY_ROWS = 12
KEY_BLOCK = KEY_ROWS * GRID_W
MASKED = -1e30
PEER_EXPERT_BLOCK = 1024
PEER_PAIR = 2 * PEER_NKEYS
PEER_KEYS_PER_PASS = 2
PEER_OUT_CHUNKS = 4
VMEM_LIMIT = 60 * 1024 * 1024
PEER_SCHEDULER_FLAGS = None

MXU_DTYPE = jnp.bfloat16
GATE_DTYPE = jnp.bfloat16
F32 = jnp.float32
NT_DIMS = (((1,), (1,)), ((), ()))


def _cparams(sem, flags=None):
    return pltpu.CompilerParams(dimension_semantics=sem, vmem_limit_bytes=VMEM_LIMIT, flags=flags)


def _rms(x, g):
    ms = jnp.mean(x * x, axis=-1, keepdims=True)
    return x * lax.rsqrt(ms + EPS) * g


def _group_rms128(y, g, lo):
    y2 = y * y
    s_lo = jnp.sum(jnp.where(lo, y2, 0.0), axis=-1, keepdims=True)
    s_hi = jnp.sum(jnp.where(lo, 0.0, y2), axis=-1, keepdims=True)
    inv = 1.0 / GROUP_DIM
    r = jnp.where(lo, lax.rsqrt(s_lo * inv + EPS), lax.rsqrt(s_hi * inv + EPS))
    return y * r * g


def _lo_mask():
    return lax.broadcasted_iota(jnp.int32, (1, LANES), 1) < GROUP_DIM


def _mod_row(mod_ref, tiles_per_batch):
    g = jnp.minimum(pl.program_id(0) // tiles_per_batch, 2)
    return mod_ref[pl.ds(g, 1), :]


def _adaln_kernel(c_ref, w_ref, b_ref, o_ref):
    cv = c_ref[...]
    s = cv * (1.0 / (1.0 + jnp.exp(-cv)))
    o_ref[...] = jnp.dot(s.astype(MXU_DTYPE), w_ref[...].astype(MXU_DTYPE),
                         preferred_element_type=F32) + b_ref[...]


def _adaln(cvec, ada_w, ada_b):
    depth, d, cols = ada_w.shape
    cb = 1536
    return pl.pallas_call(
        _adaln_kernel,
        grid=(depth, cols // cb),
        in_specs=[pl.BlockSpec((SUBLANES, d), lambda l, j: (0, 0)),
                  pl.BlockSpec((None, d, cb), lambda l, j: (l, 0, j)),
                  pl.BlockSpec((None, 1, cb), lambda l, j: (l, 0, j))],
        out_specs=pl.BlockSpec((None, SUBLANES, cb), lambda l, j: (l, 0, j)),
        out_shape=jax.ShapeDtypeStruct((depth, SUBLANES, cols), F32),
        compiler_params=_cparams(("parallel", "parallel")),
    )(cvec, ada_w, ada_b.reshape(depth, 1, cols))


def _proj_kernel(x_ref, mod_ref, n1g_ref, w_ref, sw_ref, sb_ref, sg_ref,
                 gb_ref, z_ref, ysgu_ref, qkv_ref, *, tiles_per_batch):
    d = D_MODEL
    mrow = _mod_row(mod_ref, tiles_per_batch)
    sh1, sc1 = mrow[:, 0:d], mrow[:, d:2 * d]
    h = _rms(x_ref[...], n1g_ref[...]) * (1.0 + sc1) + sh1
    p = jnp.dot(h.astype(MXU_DTYPE), w_ref[...], preferred_element_type=F32)
    gb_ref[...] = p[:, 0:CONV_W]
    z_ref[...] = p[:, CONV_W:2 * CONV_W] * p[:, 2 * CONV_W:3 * CONV_W]
    sg = jax.nn.gelu(p[:, 3 * CONV_W:MIX_COLS])
    u, v = sg[:, :SGU_W], sg[:, SGU_W:]
    lo = _lo_mask()
    vn = jnp.concatenate(
        [_group_rms128(v[:, :LANES], sg_ref[:, :LANES], lo),
         _group_rms128(v[:, LANES:], sg_ref[:, LANES:], lo)], axis=1).astype(MXU_DTYPE)
    grp = lax.broadcasted_iota(jnp.int32, (1, SGU_W), 1) // GROUP_DIM
    for c in range(TOKEN_TILE // CHUNK):
        rows = slice(c * CHUNK, (c + 1) * CHUNK)
        vc = vn[rows, :]
        mixed = jnp.zeros((CHUNK, SGU_W), F32)
        for gi in range(SGU_GROUPS):
            mg = jnp.dot(sw_ref[gi], vc, preferred_element_type=F32)
            mixed = jnp.where(grp == gi, mg, mixed)
        ysgu_ref[rows, :] = u[rows, :] * (mixed + sb_ref[...])
    qkv_ref[...] = p[:, MIX_COLS:].astype(qkv_ref.dtype)


def _proj(xa, mod, n1g, w_in, sgu_w, sgu_bias, sgu_g, tiles_per_batch):
    n, d = xa.shape
    tm = TOKEN_TILE
    row = lambda w: pl.BlockSpec((tm, w), lambda t: (t, 0))
    full = lambda a: pl.BlockSpec(a.shape, lambda t: (0,) * a.ndim)
    return pl.pallas_call(
        functools.partial(_proj_kernel, tiles_per_batch=tiles_per_batch),
        grid=(n // tm,),
        in_specs=[row(d), full(mod), full(n1g), full(w_in), full(sgu_w), full(sgu_bias), full(sgu_g)],
        out_specs=[row(CONV_W), row(CONV_W), row(SGU_W), row(3 * NA_W)],
        out_shape=[jax.ShapeDtypeStruct((n, CONV_W), F32), jax.ShapeDtypeStruct((n, CONV_W), F32),
                   jax.ShapeDtypeStruct((n, SGU_W), F32), jax.ShapeDtypeStruct((n, 3 * NA_W), MXU_DTYPE)],
        compiler_params=_cparams(("parallel",)),
    )(xa, mod, n1g, w_in, sgu_w, sgu_bias, sgu_g)


def _attn_kernel(q_ref, k_ref, v_ref, kc_ref, vc_ref, bias_ref, o_ref, *, rows):
    j = pl.program_id(2)
    kr0 = jnp.clip(QUERY_ROWS * j - NA_KH // 2, 0, rows - KEY_ROWS)
    start = pl.multiple_of(kr0 * GRID_W, GRID_W)
    kl = k_ref[pl.ds(start, KEY_BLOCK), :]
    vl = v_ref[pl.ds(start, KEY_BLOCK), :]
    kc, vc, q = kc_ref[...], vc_ref[...], q_ref[...]
    lo = _lo_mask()
    scale = GROUP_DIM ** -0.5
    o = jnp.zeros((QUERY_BLOCK, LANES), F32)
    for hh in range(2):
        hm = lo if hh == 0 else jnp.logical_not(lo)
        qh = jnp.where(hm, q, jnp.zeros_like(q)) * scale
        s_loc = lax.dot_general(qh, kl, NT_DIMS, preferred_element_type=F32) + bias_ref[0, hh]
        s_ctx = lax.dot_general(qh, kc, NT_DIMS, preferred_element_type=F32)
        m = jnp.maximum(jnp.max(s_loc, axis=-1, keepdims=True), jnp.max(s_ctx, axis=-1, keepdims=True))
        p_loc = jnp.exp(s_loc - m)
        p_ctx = jnp.exp(s_ctx - m)
        den = jnp.sum(p_loc, axis=-1, keepdims=True) + jnp.sum(p_ctx, axis=-1, keepdims=True)
        vlh = jnp.where(hm, vl, jnp.zeros_like(vl))
        vch = jnp.where(hm, vc, jnp.zeros_like(vc))
        oh = (jnp.dot(p_loc.astype(MXU_DTYPE), vlh, preferred_element_type=F32)
              + jnp.dot(p_ctx.astype(MXU_DTYPE), vch, preferred_element_type=F32))
        o = o + oh * (1.0 / den)
    o_ref[...] = o


def _attention(qkv, bias, batch, seq):
    n = qkv.shape[0]
    rows = seq // GRID_W
    nblk = rows // QUERY_ROWS
    ctx_q0 = batch * seq // QUERY_BLOCK
    heads_cols = NA_W // LANES

    def q_map(b, hp, j):
        return (jnp.where(j < nblk, b * nblk + j, ctx_q0 + b), hp)

    def variant(b, hp, j):
        v = jnp.where(j == 0, 0, jnp.where(j < nblk - 1, 1, jnp.where(j == nblk - 1, 2, 3)))
        return (v, hp, 0, 0)

    return pl.pallas_call(
        functools.partial(_attn_kernel, rows=rows),
        grid=(batch, heads_cols, nblk + 1),
        in_specs=[pl.BlockSpec((QUERY_BLOCK, LANES), q_map),
                  pl.BlockSpec((seq, LANES), lambda b, hp, j: (b, heads_cols + hp)),
                  pl.BlockSpec((seq, LANES), lambda b, hp, j: (b, 2 * heads_cols + hp)),
                  pl.BlockSpec((CTX_LEN, LANES), lambda b, hp, j: (ctx_q0 + b, heads_cols + hp)),
                  pl.BlockSpec((CTX_LEN, LANES), lambda b, hp, j: (ctx_q0 + b, 2 * heads_cols + hp)),
                  pl.BlockSpec((1, 2, QUERY_BLOCK, KEY_BLOCK), variant)],
        out_specs=pl.BlockSpec((QUERY_BLOCK, LANES), q_map),
        out_shape=jax.ShapeDtypeStruct((n, NA_W), F32),
        compiler_params=_cparams(("parallel", "parallel", "arbitrary")),
    )(qkv, qkv, qkv, qkv, qkv, bias)


def _bias_block_index(rows):
    kh = min(NA_KH, rows)
    idx = np.full((4, QUERY_ROWS, KEY_ROWS), 2 * NA_KH - 1, np.int32)
    for v, (r0, kr0) in enumerate(((0, 0), (QUERY_ROWS, 0), (rows - QUERY_ROWS, rows - KEY_ROWS))):
        for qr in range(QUERY_ROWS):
            r = r0 + qr
            rs = min(max(r - kh // 2, 0), rows - kh)
            for kr in range(KEY_ROWS):
                if rs <= kr0 + kr < rs + kh:
                    idx[v, qr, kr] = kr0 + kr - r + NA_KH - 1
    return idx


def _build_bias(rpb, block_index):
    heads = rpb.shape[0]
    pad = GRID_W - NA_KW
    padded = jnp.pad(rpb, ((0, 0), (0, 0), (pad, pad)), constant_values=MASKED)
    toe = jnp.stack([padded[:, :, GRID_W - 1 - qc:2 * GRID_W - 1 - qc] for qc in range(GRID_W)], axis=2)
    qc = np.arange(GRID_W)[:, None]
    kc = np.arange(GRID_W)[None, :]
    cs = np.clip(qc - NA_KW // 2, 0, GRID_W - NA_KW)
    toe = jnp.where((kc >= cs) & (kc < cs + NA_KW), toe, MASKED)
    toe = jnp.concatenate([toe, jnp.full((heads, 1, GRID_W, GRID_W), MASKED, toe.dtype)], axis=1)
    flat = [int(a) for a in block_index.reshape(-1)]
    blocks = jnp.stack([toe[:, a] for a in flat], axis=1)
    blocks = blocks.reshape(heads, 4, QUERY_ROWS, KEY_ROWS, GRID_W, GRID_W)
    return jnp.transpose(blocks, (1, 0, 2, 4, 3, 5)).reshape(4, heads, QUERY_BLOCK, KEY_BLOCK)


def _merge_kernel(gb_ref, z_ref, zp_ref, zn_ref, ysgu_ref, yna_ref, x_ref, mod_ref, cw_ref, og_ref,
                  w_ref, n2g_ref, xo_ref, h2_ref, *, tiles_per_batch, seq, n_latent):
    d = D_MODEL
    tm = TOKEN_TILE
    t = pl.program_id(0)
    mrow = _mod_row(mod_ref, tiles_per_batch)
    g1, sh2, sc2 = mrow[:, 2 * d:3 * d], mrow[:, 3 * d:4 * d], mrow[:, 4 * d:5 * d]
    z = z_ref[...]
    ridx = lax.broadcasted_iota(jnp.int32, (tm, 1), 0)
    grow = t * tm + ridx
    seqlen = jnp.where(t * tm >= n_latent, CTX_LEN, seq)
    pos = grow & (seqlen - 1)
    z_prev = jnp.where(ridx == 0, zp_ref[SUBLANES - 1:SUBLANES, :], pltpu.roll(z, 1, 0))
    z_next = jnp.where(ridx == tm - 1, zn_ref[0:1, :], pltpu.roll(z, tm - 1, 0))
    z_prev = jnp.where(pos == 0, 0.0, z_prev)
    z_next = jnp.where(pos == seqlen - 1, 0.0, z_next)
    conv = z_prev * cw_ref[0:1, :] + z * cw_ref[1:2, :] + z_next * cw_ref[2:3, :]
    y_conv = gb_ref[...] * conv
    lo = _lo_mask()
    blocks = []
    for src, width in ((y_conv, CONV_W), (ysgu_ref[...], SGU_W), (yna_ref[...], NA_W)):
        for c in range(width // LANES):
            off = len(blocks) * LANES
            blocks.append(_group_rms128(src[:, c * LANES:(c + 1) * LANES], og_ref[:, off:off + LANES], lo)
                          .astype(MXU_DTYPE))
    y = jnp.dot(jnp.concatenate(blocks, axis=1), w_ref[...], preferred_element_type=F32)
    xm = x_ref[...] + g1 * y
    xo_ref[...] = xm
    h2_ref[...] = (_rms(xm, n2g_ref[...]) * (1.0 + sc2) + sh2).astype(h2_ref.dtype)


def _merge(gb, z, ysgu, yna, xa, mod, conv_w, out_g, w_out, n2g, tiles_per_batch, seq, n_latent):
    n, d = xa.shape
    tm = TOKEN_TILE
    hb = tm // SUBLANES
    nb8 = n // SUBLANES
    row = lambda w: pl.BlockSpec((tm, w), lambda t: (t, 0))
    full = lambda a: pl.BlockSpec(a.shape, lambda t: (0,) * a.ndim)
    halo_prev = pl.BlockSpec((SUBLANES, CONV_W), lambda t: (jnp.maximum(t * hb - 1, 0), 0))
    halo_next = pl.BlockSpec((SUBLANES, CONV_W), lambda t: (jnp.minimum((t + 1) * hb, nb8 - 1), 0))
    return pl.pallas_call(
        functools.partial(_merge_kernel, tiles_per_batch=tiles_per_batch, seq=seq, n_latent=n_latent),
        grid=(n // tm,),
        in_specs=[row(CONV_W), row(CONV_W), halo_prev, halo_next, row(SGU_W), row(NA_W), row(d),
                  full(mod), full(conv_w), full(out_g), full(w_out), full(n2g)],
        out_specs=[row(d), row(d)],
        out_shape=[jax.ShapeDtypeStruct((n, d), F32), jax.ShapeDtypeStruct((n, d), MXU_DTYPE)],
        compiler_params=_cparams(("parallel",)),
    )(gb, z, z, z, ysgu, yna, xa, mod, conv_w, out_g, w_out, n2g)


def _sort_desc(vals):
    n = len(vals)
    size = 2
    while size <= n:
        k = size // 2
        while k >= 1:
            for i in range(n):
                p = i ^ k
                if p > i:
                    hi, lo = jnp.maximum(vals[i], vals[p]), jnp.minimum(vals[i], vals[p])
                    vals[i], vals[p] = (hi, lo) if (i & size) == 0 else (lo, hi)
            k //= 2
        size *= 2
    return vals


def _merge_bitonic_desc(vals):
    n = len(vals)
    k = n // 2
    while k >= 1:
        for i in range(n):
            if (i & k) == 0:
                hi, lo = jnp.maximum(vals[i], vals[i + k]), jnp.minimum(vals[i], vals[i + k])
                vals[i], vals[i + k] = hi, lo
        k //= 2
    return vals


def _top_merge(a, b):
    n = len(a)
    return _merge_bitonic_desc([jnp.maximum(a[i], b[n - 1 - i]) for i in range(n)])


def _top16_over_keys(tiles):
    vals = _sort_desc(list(tiles))
    for shift in (4, 2, 1):
        vals = _top_merge(vals, [pltpu.roll(v, shift, 0) for v in vals])
    return vals


def _pair_threshold(a, b):
    k = PEER_TOPK
    rows = [[a[i] + b[j] for j in range(k // (i + 1))] for i in range(k)]
    neg = jnp.full(a[0].shape, -jnp.inf, F32)
    pad = lambda vals: vals + [neg] * (k - len(vals))
    l1 = _sort_desc(pad(rows[1] + rows[2]))
    l2 = _sort_desc(pad(rows[3] + rows[4] + rows[5] + rows[6] + rows[7]))
    l3 = _sort_desc(pad([v for r in rows[8:] for v in r]))
    return _top_merge(_top_merge(rows[0], l1), _top_merge(l2, l3))


def _sorted_count(b, pred):
    one = lambda p, v: jnp.where(p, v, 0.0)
    p8 = pred(b[7])
    p4 = pred(jnp.where(p8, b[11], b[3]))
    p2 = pred(jnp.where(p8, jnp.where(p4, b[13], b[9]), jnp.where(p4, b[5], b[1])))
    lo = jnp.where(p4, jnp.where(p2, b[6], b[4]), jnp.where(p2, b[2], b[0]))
    hi = jnp.where(p4, jnp.where(p2, b[14], b[12]), jnp.where(p2, b[10], b[8]))
    p1 = pred(jnp.where(p8, hi, lo))
    p0 = pred(b[15])
    return one(p8, 8.0) + one(p4, 4.0) + one(p2, 2.0) + one(p1, 1.0) + one(p0, 1.0)


def _gelu_tanh(x):
    k1 = -2.0 * (2.0 / np.pi) ** 0.5
    k2 = k1 * 0.044715
    return x / (1.0 + jnp.exp(x * (k1 + k2 * (x * x))))


def _peer_kernel(zero_ref, h2_ref, x_ref, mod_ref, wq_ref, sk_ref, u_first_ref, u_odd_ref, u_next_ref, vt_ref,
                 o_ref, s_sc, cnt_sc, e0_sc, rk_sc, e1_sc, cnt_st, e0_st, h2t_sc, qt_sc,
                 act_e0, act_e1, act_e2, act_e3, act_o0, act_o1, act_o2, act_o3, w_e, w_o, acc_sc,
                 *, tiles_per_batch):
    d = D_MODEL
    tm = TOKEN_TILE
    nk = PEER_NKEYS
    key_tiles = nk // SUBLANES
    lane_tiles = tm // LANES
    packed = 2 * SUBLANES
    gate_tiles = nk // packed
    n_pairs = PEER_EXPERT_BLOCK // PEER_PAIR
    n_blocks = nk * nk // PEER_EXPERT_BLOCK
    e = pl.program_id(1)
    gdt = rk_sc.dtype
    assert PEER_EXPERT_BLOCK == SUBLANES * nk

    @pl.when(e == 0)
    def _scores_and_thresholds():
        h2t_sc[...] = h2_ref[...].astype(F32).T.astype(h2t_sc.dtype)
        for ip, buf in enumerate((act_e0, act_e1, act_e2, act_e3)):
            buf[0] = jnp.dot(u_first_ref[ip * PEER_PAIR:(ip + 1) * PEER_PAIR, :], h2t_sc[...],
                             preferred_element_type=F32)
        q_t = jnp.dot(wq_ref[...], h2t_sc[...], preferred_element_type=F32)
        qt_sc[...] = q_t.astype(qt_sc.dtype)
        for hp in range(2 * PEER_HEADS):
            s_t = jnp.dot(sk_ref[hp], qt_sc[hp * nk:(hp + 1) * nk, :], preferred_element_type=F32)
            for lt in range(lane_tiles):
                s_sc[hp, lt] = s_t[:, lt * LANES:(lt + 1) * LANES]
        acc_sc[...] = jnp.zeros_like(acc_sc)

        def head_tile(it, carry):
            h = it // lane_tiles
            lt = it % lane_tiles
            rows = lambda v: slice(v * SUBLANES, (v + 1) * SUBLANES)
            s0 = [s_sc[2 * h, lt, rows(v), :] for v in range(key_tiles)]
            s1 = [s_sc[2 * h + 1, lt, rows(v), :] for v in range(key_tiles)]
            a = _top16_over_keys(s0)
            b = _top16_over_keys(s1)
            best = _pair_threshold(a, b)
            tau = best[PEER_TOPK - 1]
            z = jnp.ones_like(tau)
            for kk in range(1, PEER_TOPK):
                z = z + jnp.exp(best[kk] - best[0])
            rz = 1.0 / z
            for v in range(key_tiles):
                cnt_sc[h, lt, rows(v), :] = _sorted_count(b, lambda bk: s0[v] + bk >= tau)
                e0_sc[h, lt, rows(v), :] = jnp.exp(s0[v] - a[0])
            for vp in range(gate_tiles):
                rk, e1 = [], []
                for v in (2 * vp, 2 * vp + 1):
                    rk.append(_sorted_count(b, lambda bk: bk > s1[v]))
                    e1.append(jnp.exp(s1[v] - b[0]) * rz)
                prow = slice(vp * packed, (vp + 1) * packed)
                rk_sc[h, lt, prow, :] = jnp.concatenate(rk, axis=0).astype(gdt)
                e1_sc[h, lt, prow, :] = jnp.concatenate(e1, axis=0).astype(gdt)
            return carry

        lax.fori_loop(0, PEER_HEADS * lane_tiles, head_tile, 0)

    base = pl.multiple_of(e * 2 * SUBLANES, 2 * SUBLANES)
    for h in range(PEER_HEADS):
        for lt in range(lane_tiles):
            cnt_st[h, lt] = cnt_sc[h, lt, pl.ds(base, 2 * SUBLANES), :]
            e0_st[h, lt] = e0_sc[h, lt, pl.ds(base, 2 * SUBLANES), :]
    z0 = zero_ref[0]

    def pre_activation(u_block, bufs, ip):
        bufs[ip][0] = jnp.dot(u_block[ip * PEER_PAIR:(ip + 1) * PEER_PAIR, :], h2t_sc[...],
                              preferred_element_type=F32)

    def gated_activation(s, bufs, w_buf, ip):
        kpp = PEER_KEYS_PER_PASS
        for lt in range(lane_tiles):
            lanes = slice(lt * LANES, (lt + 1) * LANES)
            for i0 in range(0, 2, kpp):
                gate = [[jnp.zeros((packed, LANES), gdt) for _ in range(gate_tiles)] for _ in range(kpp)]
                for h in range(PEER_HEADS):
                    def row(ref, ii):
                        r = s * SUBLANES + 2 * ip + i0 + ii
                        return jnp.broadcast_to(ref[h, lt, r:r + 1, :], (packed, LANES)).astype(gdt)
                    cnt = [row(cnt_st, ii) for ii in range(kpp)]
                    e0 = [row(e0_st, ii) for ii in range(kpp)]
                    for jp in range(gate_tiles):
                        prow = slice(jp * packed, (jp + 1) * packed)
                        rk = rk_sc[h, lt, prow, :]
                        e1 = e1_sc[h, lt, prow, :]
                        for ii in range(kpp):
                            gate[ii][jp] = gate[ii][jp] + jnp.where(rk < cnt[ii], e1 * e0[ii],
                                                                     jnp.zeros_like(e1))
                for ii in range(kpp):
                    for jp in range(gate_tiles):
                        r0 = (i0 + ii) * nk + jp * packed
                        act = _gelu_tanh(bufs[ip][z0, r0:r0 + packed, lanes].astype(gdt))
                        w0 = ip * PEER_PAIR + r0
                        w_buf[0, w0:w0 + packed, lanes] = (gate[ii][jp] * act).astype(w_buf.dtype)

    def output(s, w_buf):
        for ic in range(PEER_OUT_CHUNKS):
            rows = slice(ic * (d // PEER_OUT_CHUNKS), (ic + 1) * (d // PEER_OUT_CHUNKS))
            acc_sc[rows, :] += jnp.dot(vt_ref[s, rows, :], w_buf[z0], preferred_element_type=F32)

    even = (act_e0, act_e1, act_e2, act_e3)
    odd = (act_o0, act_o1, act_o2, act_o3)
    for ip in range(n_pairs):
        pre_activation(u_odd_ref, odd, ip)
        gated_activation(0, even, w_e, ip)
    output(0, w_e)
    for ip in range(n_pairs):
        pre_activation(u_next_ref, even, ip)
        gated_activation(1, odd, w_o, ip)
    output(1, w_o)

    @pl.when(e == pl.num_programs(1) - 1)
    def _residual():
        g2 = _mod_row(mod_ref, tiles_per_batch)[:, 5 * d:6 * d]
        o_ref[...] = x_ref[...] + g2 * acc_sc[...].T


def _peer(h2, xm, mod, wq_t, subkeys, u_tab, vt_tab, tiles_per_batch):
    n, d = xm.shape
    tm = TOKEN_TILE
    eb = PEER_EXPERT_BLOCK
    nb = u_tab.shape[0]
    full = lambda a: pl.BlockSpec(a.shape, lambda t, e: (0,) * a.ndim)
    row = pl.BlockSpec((tm, d), lambda t, e: (t, 0))
    once = pl.BlockSpec((tm, d), lambda t, e: (t, 0), pipeline_mode=pl.Buffered(1))
    u_block = lambda imap: pl.BlockSpec((None, eb, d), imap)
    hp2 = 2 * PEER_HEADS
    lt = tm // LANES
    return pl.pallas_call(
        functools.partial(_peer_kernel, tiles_per_batch=tiles_per_batch),
        grid=(n // tm, nb // 2),
        in_specs=[pl.BlockSpec(memory_space=pltpu.SMEM), once, once, full(mod), full(wq_t), full(subkeys),
                  u_block(lambda t, e: (0, 0, 0)),
                  u_block(lambda t, e: (2 * e + 1, 0, 0)),
                  u_block(lambda t, e: (jnp.minimum(2 * e + 2, nb - 1), 0, 0)),
                  pl.BlockSpec((2, d, eb), lambda t, e: (e, 0, 0))],
        out_specs=row,
        out_shape=jax.ShapeDtypeStruct((n, d), F32),
        scratch_shapes=[pltpu.VMEM((hp2, lt, PEER_NKEYS, LANES), F32),
                        pltpu.VMEM((PEER_HEADS, lt, PEER_NKEYS, LANES), F32),
                        pltpu.VMEM((PEER_HEADS, lt, PEER_NKEYS, LANES), F32),
                        pltpu.VMEM((PEER_HEADS, lt, PEER_NKEYS, LANES), GATE_DTYPE),
                        pltpu.VMEM((PEER_HEADS, lt, PEER_NKEYS, LANES), GATE_DTYPE),
                        pltpu.VMEM((PEER_HEADS, lt, 2 * SUBLANES, LANES), F32),
                        pltpu.VMEM((PEER_HEADS, lt, 2 * SUBLANES, LANES), F32),
                        pltpu.VMEM((d, tm), MXU_DTYPE),
                        pltpu.VMEM((hp2 * PEER_NKEYS, tm), MXU_DTYPE)]
                       + [pltpu.VMEM((1, PEER_PAIR, tm), F32)] * 8
                       + [pltpu.VMEM((1, eb, tm), MXU_DTYPE)] * 2
                       + [pltpu.VMEM((d, tm), F32)],
        compiler_params=_cparams(("parallel", "arbitrary"), PEER_SCHEDULER_FLAGS),
    )(jnp.zeros((1,), jnp.int32), h2, xm, mod, wq_t, subkeys, u_tab, u_tab, u_tab, vt_tab)


def _final_kernel(x_ref, g_ref, o_ref):
    o_ref[...] = _rms(x_ref[...], g_ref[...])


def _final_norm(xa, g, n_latent):
    d = xa.shape[1]
    tm = TOKEN_TILE
    return pl.pallas_call(
        _final_kernel,
        grid=(n_latent // tm,),
        in_specs=[pl.BlockSpec((tm, d), lambda t: (t, 0)), pl.BlockSpec((1, d), lambda t: (0, 0))],
        out_specs=pl.BlockSpec((tm, d), lambda t: (t, 0)),
        out_shape=jax.ShapeDtypeStruct((n_latent, d), F32),
        compiler_params=_cparams(("parallel",)),
    )(xa, g)


def kernel(x, c, ctx, c_ctx, ada_w, ada_b, norm1_g, w_in, conv_w, sgu_w, sgu_b, sgu_norm_g, na_rpb,
           out_norm_g, w_out, norm2_g, peer_wq, peer_subkeys, peer_u, peer_v, final_norm_g):
    batch, seq, d = x.shape
    depth = ada_w.shape[0]
    assert d == D_MODEL and batch == 2 and ctx.shape[1] == CTX_LEN
    assert seq % TOKEN_TILE == 0 and batch * CTX_LEN == TOKEN_TILE
    rows = seq // GRID_W
    assert rows % QUERY_ROWS == 0 and rows >= KEY_ROWS + QUERY_ROWS
    n_latent = batch * seq
    tiles_per_batch = seq // TOKEN_TILE

    xa = jnp.concatenate([x.reshape(n_latent, d), ctx.reshape(batch * CTX_LEN, d)], axis=0)
    cvec = jnp.zeros((SUBLANES, d), F32).at[0:batch].set(c).at[batch].set(c_ctx)
    mods = _adaln(cvec, ada_w, ada_b)
    block_index = _bias_block_index(rows)

    mx = MXU_DTYPE
    w_in_c = w_in.astype(mx)
    w_out_c = w_out.astype(mx)
    sgu_w_c = sgu_w.astype(mx)
    sgu_bias = jnp.repeat(jnp.swapaxes(sgu_b, 1, 2), GROUP_DIM, axis=2)
    wq_t = jnp.swapaxes(peer_wq, 1, 2).astype(mx)
    subkeys = peer_subkeys.reshape(depth, 2 * PEER_HEADS, PEER_NKEYS, PEER_DKEY // 2).astype(mx)
    n_blocks = peer_u.shape[1] // PEER_EXPERT_BLOCK
    u_c = peer_u.astype(mx).reshape(depth, n_blocks, PEER_EXPERT_BLOCK, d)
    vt_c = jnp.swapaxes(peer_v.astype(mx).reshape(depth, n_blocks, PEER_EXPERT_BLOCK, d), 2, 3)

    for l in range(depth):
        mod = mods[l]
        gb, z, ysgu, qkv = _proj(xa, mod, norm1_g[l][None], w_in_c[l], sgu_w_c[l], sgu_bias[l],
                                 sgu_norm_g[l][None], tiles_per_batch)
        yna = _attention(qkv, _build_bias(na_rpb[l], block_index), batch, seq)
        xm, h2 = _merge(gb, z, ysgu, yna, xa, mod, conv_w[l], out_norm_g[l][None], w_out_c[l],
                        norm2_g[l][None], tiles_per_batch, seq, n_latent)
        xa = _peer(h2, xm, mod, wq_t[l], subkeys[l], u_c[l], vt_c[l], tiles_per_batch)
    return _final_norm(xa, final_norm_g[None], n_latent).reshape(batch, seq, d)
```

```python
import functools

import numpy as np
import jax
import jax.numpy as jnp
from jax import lax
from jax.experimental import pallas as pl
from jax.experimental.pallas import tpu as pltpu

D_MODEL = 1024
GRID_W = 64
CTX_LEN = 256
EPS = 1e-6
GROUP_DIM = 64
CONV_W = 256
SGU_W = 256
SGU_GROUPS = 4
CHUNK = 128
NA_HEADS = 8
NA_W = 512
NA_KH = 8
NA_KW = 16
MIX_COLS = 3 * CONV_W + 2 * SGU_W
IN_W = MIX_COLS + 3 * NA_W
PEER_HEADS = 8
PEER_NKEYS = 128
PEER_TOPK = 16
PEER_DKEY = 256

LANES = 128
SUBLANES = 8
TOKEN_TILE = 512
QUERY_ROWS = 4
QUERY_BLOCK = QUERY_ROWS * GRID_W
KEY_ROWS = 12
KEY_BLOCK = KEY_ROWS * GRID_W
MASKED = -1e30
PEER_EXPERT_BLOCK = 1024
PEER_PAIR = 2 * PEER_NKEYS
PEER_KEYS_PER_PASS = 2
PEER_OUT_CHUNKS = 4
VMEM_LIMIT = 60 * 1024 * 1024
PEER_SCHEDULER_FLAGS = None

MXU_DTYPE = jnp.bfloat16
GATE_DTYPE = jnp.bfloat16
F32 = jnp.float32
NT_DIMS = (((1,), (1,)), ((), ()))


def _cparams(sem, flags=None):
    return pltpu.CompilerParams(dimension_semantics=sem, vmem_limit_bytes=VMEM_LIMIT, flags=flags)


def _rms(x, g):
    ms = jnp.mean(x * x, axis=-1, keepdims=True)
    return x * lax.rsqrt(ms + EPS) * g


def _group_rms128(y, g, lo):
    y2 = y * y
    s_lo = jnp.sum(jnp.where(lo, y2, 0.0), axis=-1, keepdims=True)
    s_hi = jnp.sum(jnp.where(lo, 0.0, y2), axis=-1, keepdims=True)
    inv = 1.0 / GROUP_DIM
    r = jnp.where(lo, lax.rsqrt(s_lo * inv + EPS), lax.rsqrt(s_hi * inv + EPS))
    return y * r * g


def _lo_mask():
    return lax.broadcasted_iota(jnp.int32, (1, LANES), 1) < GROUP_DIM


def _mod_row(mod_ref, tiles_per_batch):
    g = jnp.minimum(pl.program_id(0) // tiles_per_batch, 2)
    return mod_ref[pl.ds(g, 1), :]


def _adaln_kernel(c_ref, w_ref, b_ref, o_ref):
    cv = c_ref[...]
    s = cv * (1.0 / (1.0 + jnp.exp(-cv)))
    o_ref[...] = jnp.dot(s.astype(MXU_DTYPE), w_ref[...].astype(MXU_DTYPE),
                         preferred_element_type=F32) + b_ref[...]


def _adaln(cvec, ada_w, ada_b):
    depth, d, cols = ada_w.shape
    cb = 1536
    return pl.pallas_call(
        _adaln_kernel,
        grid=(depth, cols // cb),
        in_specs=[pl.BlockSpec((SUBLANES, d), lambda l, j: (0, 0)),
                  pl.BlockSpec((None, d, cb), lambda l, j: (l, 0, j)),
                  pl.BlockSpec((None, 1, cb), lambda l, j: (l, 0, j))],
        out_specs=pl.BlockSpec((None, SUBLANES, cb), lambda l, j: (l, 0, j)),
        out_shape=jax.ShapeDtypeStruct((depth, SUBLANES, cols), F32),
        compiler_params=_cparams(("parallel", "parallel")),
    )(cvec, ada_w, ada_b.reshape(depth, 1, cols))


def _proj_kernel(x_ref, mod_ref, n1g_ref, w_ref, sw_ref, sb_ref, sg_ref,
                 gb_ref, z_ref, ysgu_ref, qkv_ref, *, tiles_per_batch):
    d = D_MODEL
    mrow = _mod_row(mod_ref, tiles_per_batch)
    sh1, sc1 = mrow[:, 0:d], mrow[:, d:2 * d]
    h = _rms(x_ref[...], n1g_ref[...]) * (1.0 + sc1) + sh1
    p = jnp.dot(h.astype(MXU_DTYPE), w_ref[...], preferred_element_type=F32)
    gb_ref[...] = p[:, 0:CONV_W]
    z_ref[...] = p[:, CONV_W:2 * CONV_W] * p[:, 2 * CONV_W:3 * CONV_W]
    sg = jax.nn.gelu(p[:, 3 * CONV_W:MIX_COLS])
    u, v = sg[:, :SGU_W], sg[:, SGU_W:]
    lo = _lo_mask()
    vn = jnp.concatenate(
        [_group_rms128(v[:, :LANES], sg_ref[:, :LANES], lo),
         _group_rms128(v[:, LANES:], sg_ref[:, LANES:], lo)], axis=1).astype(MXU_DTYPE)
    grp = lax.broadcasted_iota(jnp.int32, (1, SGU_W), 1) // GROUP_DIM
    for c in range(TOKEN_TILE // CHUNK):
        rows = slice(c * CHUNK, (c + 1) * CHUNK)
        vc = vn[rows, :]
        mixed = jnp.zeros((CHUNK, SGU_W), F32)
        for gi in range(SGU_GROUPS):
            mg = jnp.dot(sw_ref[gi], vc, preferred_element_type=F32)
            mixed = jnp.where(grp == gi, mg, mixed)
        ysgu_ref[rows, :] = u[rows, :] * (mixed + sb_ref[...])
    qkv_ref[...] = p[:, MIX_COLS:].astype(qkv_ref.dtype)


def _proj(xa, mod, n1g, w_in, sgu_w, sgu_bias, sgu_g, tiles_per_batch):
    n, d = xa.shape
    tm = TOKEN_TILE
    row = lambda w: pl.BlockSpec((tm, w), lambda t: (t, 0))
    full = lambda a: pl.BlockSpec(a.shape, lambda t: (0,) * a.ndim)
    return pl.pallas_call(
        functools.partial(_proj_kernel, tiles_per_batch=tiles_per_batch),
        grid=(n // tm,),
        in_specs=[row(d), full(mod), full(n1g), full(w_in), full(sgu_w), full(sgu_bias), full(sgu_g)],
        out_specs=[row(CONV_W), row(CONV_W), row(SGU_W), row(3 * NA_W)],
        out_shape=[jax.ShapeDtypeStruct((n, CONV_W), F32), jax.ShapeDtypeStruct((n, CONV_W), F32),
                   jax.ShapeDtypeStruct((n, SGU_W), F32), jax.ShapeDtypeStruct((n, 3 * NA_W), MXU_DTYPE)],
        compiler_params=_cparams(("parallel",)),
    )(xa, mod, n1g, w_in, sgu_w, sgu_bias, sgu_g)


def _attn_kernel(q_ref, k_ref, v_ref, kc_ref, vc_ref, bias_ref, o_ref, *, rows):
    j = pl.program_id(2)
    kr0 = jnp.clip(QUERY_ROWS * j - NA_KH // 2, 0, rows - KEY_ROWS)
    start = pl.multiple_of(kr0 * GRID_W, GRID_W)
    kl = k_ref[pl.ds(start, KEY_BLOCK), :]
    vl = v_ref[pl.ds(start, KEY_BLOCK), :]
    kc, vc, q = kc_ref[...], vc_ref[...], q_ref[...]
    lo = _lo_mask()
    scale = GROUP_DIM ** -0.5
    o = jnp.zeros((QUERY_BLOCK, LANES), F32)
    for hh in range(2):
        hm = lo if hh == 0 else jnp.logical_not(lo)
        qh = jnp.where(hm, q, jnp.zeros_like(q)) * scale
        s_loc = lax.dot_general(qh, kl, NT_DIMS, preferred_element_type=F32) + bias_ref[0, hh]
        s_ctx = lax.dot_general(qh, kc, NT_DIMS, preferred_element_type=F32)
        m = jnp.maximum(jnp.max(s_loc, axis=-1, keepdims=True), jnp.max(s_ctx, axis=-1, keepdims=True))
        p_loc = jnp.exp(s_loc - m)
        p_ctx = jnp.exp(s_ctx - m)
        den = jnp.sum(p_loc, axis=-1, keepdims=True) + jnp.sum(p_ctx, axis=-1, keepdims=True)
        vlh = jnp.where(hm, vl, jnp.zeros_like(vl))
        vch = jnp.where(hm, vc, jnp.zeros_like(vc))
        oh = (jnp.dot(p_loc.astype(MXU_DTYPE), vlh, preferred_element_type=F32)
              + jnp.dot(p_ctx.astype(MXU_DTYPE), vch, preferred_element_type=F32))
        o = o + oh * (1.0 / den)
    o_ref[...] = o


def _attention(qkv, bias, layer, batch, seq):
    n = qkv.shape[0]
    rows = seq // GRID_W
    nblk = rows // QUERY_ROWS
    ctx_q0 = batch * seq // QUERY_BLOCK
    heads_cols = NA_W // LANES

    def q_map(b, hp, j):
        return (jnp.where(j < nblk, b * nblk + j, ctx_q0 + b), hp)

    def variant(b, hp, j):
        v = jnp.where(j == 0, 0, jnp.where(j < nblk - 1, 1, jnp.where(j == nblk - 1, 2, 3)))
        return (layer, v, hp, 0, 0)

    return pl.pallas_call(
        functools.partial(_attn_kernel, rows=rows),
        grid=(batch, heads_cols, nblk + 1),
        in_specs=[pl.BlockSpec((QUERY_BLOCK, LANES), q_map),
                  pl.BlockSpec((seq, LANES), lambda b, hp, j: (b, heads_cols + hp)),
                  pl.BlockSpec((seq, LANES), lambda b, hp, j: (b, 2 * heads_cols + hp)),
                  pl.BlockSpec((CTX_LEN, LANES), lambda b, hp, j: (ctx_q0 + b, heads_cols + hp)),
                  pl.BlockSpec((CTX_LEN, LANES), lambda b, hp, j: (ctx_q0 + b, 2 * heads_cols + hp)),
                  pl.BlockSpec((None, 1, 2, QUERY_BLOCK, KEY_BLOCK), variant)],
        out_specs=pl.BlockSpec((QUERY_BLOCK, LANES), q_map),
        out_shape=jax.ShapeDtypeStruct((n, NA_W), F32),
        compiler_params=_cparams(("parallel", "parallel", "arbitrary")),
    )(qkv, qkv, qkv, qkv, qkv, bias)


def _bias_block_index(rows):
    kh = min(NA_KH, rows)
    idx = np.full((4, QUERY_ROWS, KEY_ROWS), 2 * NA_KH - 1, np.int32)
    for v, (r0, kr0) in enumerate(((0, 0), (QUERY_ROWS, 0), (rows - QUERY_ROWS, rows - KEY_ROWS))):
        for qr in range(QUERY_ROWS):
            r = r0 + qr
            rs = min(max(r - kh // 2, 0), rows - kh)
            for kr in range(KEY_ROWS):
                if rs <= kr0 + kr < rs + kh:
                    idx[v, qr, kr] = kr0 + kr - r + NA_KH - 1
    return idx


def _build_bias(rpb, block_index):
    depth, heads = rpb.shape[:2]
    pad = GRID_W - NA_KW
    padded = jnp.pad(rpb, ((0, 0), (0, 0), (0, 0), (pad, pad)), constant_values=MASKED)
    toe = jnp.stack([padded[..., GRID_W - 1 - qc:2 * GRID_W - 1 - qc] for qc in range(GRID_W)], axis=3)
    qc = np.arange(GRID_W)[:, None]
    kc = np.arange(GRID_W)[None, :]
    cs = np.clip(qc - NA_KW // 2, 0, GRID_W - NA_KW)
    toe = jnp.where((kc >= cs) & (kc < cs + NA_KW), toe, MASKED)
    toe = jnp.concatenate([toe, jnp.full((depth, heads, 1, GRID_W, GRID_W), MASKED, toe.dtype)], axis=2)
    flat = [int(a) for a in block_index.reshape(-1)]
    blocks = jnp.stack([toe[:, :, a] for a in flat], axis=2)
    blocks = blocks.reshape(depth, heads, 4, QUERY_ROWS, KEY_ROWS, GRID_W, GRID_W)
    return jnp.transpose(blocks, (0, 2, 1, 3, 5, 4, 6)).reshape(depth, 4, heads, QUERY_BLOCK, KEY_BLOCK)


def _merge_kernel(gb_ref, z_ref, zp_ref, zn_ref, ysgu_ref, yna_ref, x_ref, mod_ref, cw_ref, og_ref,
                  w_ref, n2g_ref, xo_ref, h2_ref, *, tiles_per_batch, seq, n_latent):
    d = D_MODEL
    tm = TOKEN_TILE
    t = pl.program_id(0)
    mrow = _mod_row(mod_ref, tiles_per_batch)
    g1, sh2, sc2 = mrow[:, 2 * d:3 * d], mrow[:, 3 * d:4 * d], mrow[:, 4 * d:5 * d]
    z = z_ref[...]
    ridx = lax.broadcasted_iota(jnp.int32, (tm, 1), 0)
    grow = t * tm + ridx
    seqlen = jnp.where(t * tm >= n_latent, CTX_LEN, seq)
    pos = grow & (seqlen - 1)
    z_prev = jnp.where(ridx == 0, zp_ref[SUBLANES - 1:SUBLANES, :], pltpu.roll(z, 1, 0))
    z_next = jnp.where(ridx == tm - 1, zn_ref[0:1, :], pltpu.roll(z, tm - 1, 0))
    z_prev = jnp.where(pos == 0, 0.0, z_prev)
    z_next = jnp.where(pos == seqlen - 1, 0.0, z_next)
    conv = z_prev * cw_ref[0:1, :] + z * cw_ref[1:2, :] + z_next * cw_ref[2:3, :]
    y_conv = gb_ref[...] * conv
    lo = _lo_mask()
    blocks = []
    for src, width in ((y_conv, CONV_W), (ysgu_ref[...], SGU_W), (yna_ref[...], NA_W)):
        for c in range(width // LANES):
            off = len(blocks) * LANES
            blocks.append(_group_rms128(src[:, c * LANES:(c + 1) * LANES], og_ref[:, off:off + LANES], lo)
                          .astype(MXU_DTYPE))
    y = jnp.dot(jnp.concatenate(blocks, axis=1), w_ref[...], preferred_element_type=F32)
    xm = x_ref[...] + g1 * y
    xo_ref[...] = xm
    h2_ref[...] = (_rms(xm, n2g_ref[...]) * (1.0 + sc2) + sh2).astype(h2_ref.dtype)


def _merge(gb, z, ysgu, yna, xa, mod, conv_w, out_g, w_out, n2g, tiles_per_batch, seq, n_latent):
    n, d = xa.shape
    tm = TOKEN_TILE
    hb = tm // SUBLANES
    nb8 = n // SUBLANES
    row = lambda w: pl.BlockSpec((tm, w), lambda t: (t, 0))
    full = lambda a: pl.BlockSpec(a.shape, lambda t: (0,) * a.ndim)
    halo_prev = pl.BlockSpec((SUBLANES, CONV_W), lambda t: (jnp.maximum(t * hb - 1, 0), 0))
    halo_next = pl.BlockSpec((SUBLANES, CONV_W), lambda t: (jnp.minimum((t + 1) * hb, nb8 - 1), 0))
    return pl.pallas_call(
        functools.partial(_merge_kernel, tiles_per_batch=tiles_per_batch, seq=seq, n_latent=n_latent),
        grid=(n // tm,),
        in_specs=[row(CONV_W), row(CONV_W), halo_prev, halo_next, row(SGU_W), row(NA_W), row(d),
                  full(mod), full(conv_w), full(out_g), full(w_out), full(n2g)],
        out_specs=[row(d), row(d)],
        out_shape=[jax.ShapeDtypeStruct((n, d), F32), jax.ShapeDtypeStruct((n, d), MXU_DTYPE)],
        compiler_params=_cparams(("parallel",)),
    )(gb, z, z, z, ysgu, yna, xa, mod, conv_w, out_g, w_out, n2g)


def _sort_desc(vals):
    n = len(vals)
    size = 2
    while size <= n:
        k = size // 2
        while k >= 1:
            for i in range(n):
                p = i ^ k
                if p > i:
                    hi, lo = jnp.maximum(vals[i], vals[p]), jnp.minimum(vals[i], vals[p])
                    vals[i], vals[p] = (hi, lo) if (i & size) == 0 else (lo, hi)
            k //= 2
        size *= 2
    return vals


def _merge_bitonic_desc(vals):
    n = len(vals)
    k = n // 2
    while k >= 1:
        for i in range(n):
            if (i & k) == 0:
                hi, lo = jnp.maximum(vals[i], vals[i + k]), jnp.minimum(vals[i], vals[i + k])
                vals[i], vals[i + k] = hi, lo
        k //= 2
    return vals


def _top_merge(a, b):
    n = len(a)
    return _merge_bitonic_desc([jnp.maximum(a[i], b[n - 1 - i]) for i in range(n)])


def _top16_over_keys(tiles):
    vals = _sort_desc(list(tiles))
    for shift in (4, 2, 1):
        vals = _top_merge(vals, [pltpu.roll(v, shift, 0) for v in vals])
    return vals


def _pair_threshold(a, b):
    k = PEER_TOPK
    rows = [[a[i] + b[j] for j in range(k // (i + 1))] for i in range(k)]
    neg = jnp.full(a[0].shape, -jnp.inf, F32)
    pad = lambda vals: vals + [neg] * (k - len(vals))
    l1 = _sort_desc(pad(rows[1] + rows[2]))
    l2 = _sort_desc(pad(rows[3] + rows[4] + rows[5] + rows[6] + rows[7]))
    l3 = _sort_desc(pad([v for r in rows[8:] for v in r]))
    return _top_merge(_top_merge(rows[0], l1), _top_merge(l2, l3))


def _sorted_count(b, pred):
    one = lambda p, v: jnp.where(p, v, 0.0)
    p8 = pred(b[7])
    p4 = pred(jnp.where(p8, b[11], b[3]))
    p2 = pred(jnp.where(p8, jnp.where(p4, b[13], b[9]), jnp.where(p4, b[5], b[1])))
    lo = jnp.where(p4, jnp.where(p2, b[6], b[4]), jnp.where(p2, b[2], b[0]))
    hi = jnp.where(p4, jnp.where(p2, b[14], b[12]), jnp.where(p2, b[10], b[8]))
    p1 = pred(jnp.where(p8, hi, lo))
    p0 = pred(b[15])
    return one(p8, 8.0) + one(p4, 4.0) + one(p2, 2.0) + one(p1, 1.0) + one(p0, 1.0)


def _gelu_tanh(x):
    k1 = -2.0 * (2.0 / np.pi) ** 0.5
    k2 = k1 * 0.044715
    return x / (1.0 + jnp.exp(x * (k1 + k2 * (x * x))))


def _peer_kernel(zero_ref, h2_ref, x_ref, mod_ref, wq_ref, sk_ref, u_first_ref, u_odd_ref, u_next_ref, vt_ref,
                 o_ref, s_sc, cnt_sc, e0_sc, rk_sc, e1_sc, cnt_st, e0_st, h2t_sc, qt_sc,
                 act_e0, act_e1, act_e2, act_e3, act_o0, act_o1, act_o2, act_o3, w_e, w_o, acc_sc,
                 *, tiles_per_batch):
    d = D_MODEL
    tm = TOKEN_TILE
    nk = PEER_NKEYS
    key_tiles = nk // SUBLANES
    lane_tiles = tm // LANES
    packed = 2 * SUBLANES
    gate_tiles = nk // packed
    n_pairs = PEER_EXPERT_BLOCK // PEER_PAIR
    n_blocks = nk * nk // PEER_EXPERT_BLOCK
    e = pl.program_id(1)
    gdt = rk_sc.dtype
    assert PEER_EXPERT_BLOCK == SUBLANES * nk

    @pl.when(e == 0)
    def _scores_and_thresholds():
        h2t_sc[...] = h2_ref[...].astype(F32).T.astype(h2t_sc.dtype)
        for ip, buf in enumerate((act_e0, act_e1, act_e2, act_e3)):
            buf[0] = jnp.dot(u_first_ref[ip * PEER_PAIR:(ip + 1) * PEER_PAIR, :], h2t_sc[...],
                             preferred_element_type=F32)
        q_t = jnp.dot(wq_ref[...], h2t_sc[...], preferred_element_type=F32)
        qt_sc[...] = q_t.astype(qt_sc.dtype)
        for hp in range(2 * PEER_HEADS):
            s_t = jnp.dot(sk_ref[hp], qt_sc[hp * nk:(hp + 1) * nk, :], preferred_element_type=F32)
            for lt in range(lane_tiles):
                s_sc[hp, lt] = s_t[:, lt * LANES:(lt + 1) * LANES]
        acc_sc[...] = jnp.zeros_like(acc_sc)

        def head_tile(it, carry):
            h = it // lane_tiles
            lt = it % lane_tiles
            rows = lambda v: slice(v * SUBLANES, (v + 1) * SUBLANES)
            s0 = [s_sc[2 * h, lt, rows(v), :] for v in range(key_tiles)]
            s1 = [s_sc[2 * h + 1, lt, rows(v), :] for v in range(key_tiles)]
            a = _top16_over_keys(s0)
            b = _top16_over_keys(s1)
            best = _pair_threshold(a, b)
            tau = best[PEER_TOPK - 1]
            z = jnp.ones_like(tau)
            for kk in range(1, PEER_TOPK):
                z = z + jnp.exp(best[kk] - best[0])
            rz = 1.0 / z
            for v in range(key_tiles):
                cnt_sc[h, lt, rows(v), :] = _sorted_count(b, lambda bk: s0[v] + bk >= tau)
                e0_sc[h, lt, rows(v), :] = jnp.exp(s0[v] - a[0])
            for vp in range(gate_tiles):
                rk, e1 = [], []
                for v in (2 * vp, 2 * vp + 1):
                    rk.append(_sorted_count(b, lambda bk: bk > s1[v]))
                    e1.append(jnp.exp(s1[v] - b[0]) * rz)
                prow = slice(vp * packed, (vp + 1) * packed)
                rk_sc[h, lt, prow, :] = jnp.concatenate(rk, axis=0).astype(gdt)
                e1_sc[h, lt, prow, :] = jnp.concatenate(e1, axis=0).astype(gdt)
            return carry

        lax.fori_loop(0, PEER_HEADS * lane_tiles, head_tile, 0)

    base = pl.multiple_of(e * 2 * SUBLANES, 2 * SUBLANES)
    for h in range(PEER_HEADS):
        for lt in range(lane_tiles):
            cnt_st[h, lt] = cnt_sc[h, lt, pl.ds(base, 2 * SUBLANES), :]
            e0_st[h, lt] = e0_sc[h, lt, pl.ds(base, 2 * SUBLANES), :]
    z0 = zero_ref[0]

    def pre_activation(u_block, bufs, ip):
        bufs[ip][0] = jnp.dot(u_block[ip * PEER_PAIR:(ip + 1) * PEER_PAIR, :], h2t_sc[...],
                              preferred_element_type=F32)

    def gated_activation(s, bufs, w_buf, ip):
        kpp = PEER_KEYS_PER_PASS
        for lt in range(lane_tiles):
            lanes = slice(lt * LANES, (lt + 1) * LANES)
            for i0 in range(0, 2, kpp):
                gate = [[jnp.zeros((packed, LANES), gdt) for _ in range(gate_tiles)] for _ in range(kpp)]
                for h in range(PEER_HEADS):
                    def row(ref, ii):
                        r = s * SUBLANES + 2 * ip + i0 + ii
                        return jnp.broadcast_to(ref[h, lt, r:r + 1, :], (packed, LANES)).astype(gdt)
                    cnt = [row(cnt_st, ii) for ii in range(kpp)]
                    e0 = [row(e0_st, ii) for ii in range(kpp)]
                    for jp in range(gate_tiles):
                        prow = slice(jp * packed, (jp + 1) * packed)
                        rk = rk_sc[h, lt, prow, :]
                        e1 = e1_sc[h, lt, prow, :]
                        for ii in range(kpp):
                            gate[ii][jp] = gate[ii][jp] + jnp.where(rk < cnt[ii], e1 * e0[ii],
                                                                     jnp.zeros_like(e1))
                for ii in range(kpp):
                    for jp in range(gate_tiles):
                        r0 = (i0 + ii) * nk + jp * packed
                        act = _gelu_tanh(bufs[ip][z0, r0:r0 + packed, lanes].astype(gdt))
                        w0 = ip * PEER_PAIR + r0
                        w_buf[0, w0:w0 + packed, lanes] = (gate[ii][jp] * act).astype(w_buf.dtype)

    def output(s, w_buf):
        for ic in range(PEER_OUT_CHUNKS):
            rows = slice(ic * (d // PEER_OUT_CHUNKS), (ic + 1) * (d // PEER_OUT_CHUNKS))
            acc_sc[rows, :] += jnp.dot(vt_ref[s, rows, :], w_buf[z0], preferred_element_type=F32)

    even = (act_e0, act_e1, act_e2, act_e3)
    odd = (act_o0, act_o1, act_o2, act_o3)
    for ip in range(n_pairs):
        pre_activation(u_odd_ref, odd, ip)
        gated_activation(0, even, w_e, ip)
    output(0, w_e)
    for ip in range(n_pairs):
        pre_activation(u_next_ref, even, ip)
        gated_activation(1, odd, w_o, ip)
    output(1, w_o)

    @pl.when(e == pl.num_programs(1) - 1)
    def _residual():
        g2 = _mod_row(mod_ref, tiles_per_batch)[:, 5 * d:6 * d]
        o_ref[...] = x_ref[...] + g2 * acc_sc[...].T


def _peer(h2, xm, mod, wq_t, subkeys, u_tab, vt_tab, layer, tiles_per_batch):
    n, d = xm.shape
    tm = TOKEN_TILE
    eb = PEER_EXPERT_BLOCK
    nb = u_tab.shape[1]
    full = lambda a: pl.BlockSpec(a.shape, lambda t, e: (0,) * a.ndim)
    row = pl.BlockSpec((tm, d), lambda t, e: (t, 0))
    once = pl.BlockSpec((tm, d), lambda t, e: (t, 0), pipeline_mode=pl.Buffered(1))
    u_block = lambda blk: pl.BlockSpec((None, None, eb, d), lambda t, e: (layer, blk(e), 0, 0))
    hp2 = 2 * PEER_HEADS
    lt = tm // LANES
    return pl.pallas_call(
        functools.partial(_peer_kernel, tiles_per_batch=tiles_per_batch),
        grid=(n // tm, nb // 2),
        in_specs=[pl.BlockSpec(memory_space=pltpu.SMEM), once, once, full(mod),
                  pl.BlockSpec((None,) + wq_t.shape[1:], lambda t, e: (layer, 0, 0)), full(subkeys),
                  u_block(lambda e: 0),
                  u_block(lambda e: 2 * e + 1),
                  u_block(lambda e: jnp.minimum(2 * e + 2, nb - 1)),
                  pl.BlockSpec((None, 2, d, eb), lambda t, e: (layer, e, 0, 0))],
        out_specs=row,
        out_shape=jax.ShapeDtypeStruct((n, d), F32),
        scratch_shapes=[pltpu.VMEM((hp2, lt, PEER_NKEYS, LANES), F32),
                        pltpu.VMEM((PEER_HEADS, lt, PEER_NKEYS, LANES), F32),
                        pltpu.VMEM((PEER_HEADS, lt, PEER_NKEYS, LANES), F32),
                        pltpu.VMEM((PEER_HEADS, lt, PEER_NKEYS, LANES), GATE_DTYPE),
                        pltpu.VMEM((PEER_HEADS, lt, PEER_NKEYS, LANES), GATE_DTYPE),
                        pltpu.VMEM((PEER_HEADS, lt, 2 * SUBLANES, LANES), F32),
                        pltpu.VMEM((PEER_HEADS, lt, 2 * SUBLANES, LANES), F32),
                        pltpu.VMEM((d, tm), MXU_DTYPE),
                        pltpu.VMEM((hp2 * PEER_NKEYS, tm), MXU_DTYPE)]
                       + [pltpu.VMEM((1, PEER_PAIR, tm), F32)] * 8
                       + [pltpu.VMEM((1, eb, tm), MXU_DTYPE)] * 2
                       + [pltpu.VMEM((d, tm), F32)],
        compiler_params=_cparams(("parallel", "arbitrary"), PEER_SCHEDULER_FLAGS),
    )(jnp.zeros((1,), jnp.int32), h2, xm, mod, wq_t, subkeys, u_tab, u_tab, u_tab, vt_tab)


def _final_kernel(x_ref, g_ref, o_ref):
    o_ref[...] = _rms(x_ref[...], g_ref[...])


def _final_norm(xa, g, n_latent):
    d = xa.shape[1]
    tm = TOKEN_TILE
    return pl.pallas_call(
        _final_kernel,
        grid=(n_latent // tm,),
        in_specs=[pl.BlockSpec((tm, d), lambda t: (t, 0)), pl.BlockSpec((1, d), lambda t: (0, 0))],
        out_specs=pl.BlockSpec((tm, d), lambda t: (t, 0)),
        out_shape=jax.ShapeDtypeStruct((n_latent, d), F32),
        compiler_params=_cparams(("parallel",)),
    )(xa, g)


def kernel(x, c, ctx, c_ctx, ada_w, ada_b, norm1_g, w_in, conv_w, sgu_w, sgu_b, sgu_norm_g, na_rpb,
           out_norm_g, w_out, norm2_g, peer_wq, peer_subkeys, peer_u, peer_v, final_norm_g):
    batch, seq, d = x.shape
    depth = ada_w.shape[0]
    assert d == D_MODEL and batch == 2 and ctx.shape[1] == CTX_LEN
    assert seq % TOKEN_TILE == 0 and batch * CTX_LEN == TOKEN_TILE
    rows = seq // GRID_W
    assert rows % QUERY_ROWS == 0 and rows >= KEY_ROWS + QUERY_ROWS
    n_latent = batch * seq
    tiles_per_batch = seq // TOKEN_TILE

    xa = jnp.concatenate([x.reshape(n_latent, d), ctx.reshape(batch * CTX_LEN, d)], axis=0)
    cvec = jnp.zeros((SUBLANES, d), F32).at[0:batch].set(c).at[batch].set(c_ctx)
    mods = _adaln(cvec, ada_w, ada_b)
    bias = _build_bias(na_rpb, _bias_block_index(rows))

    mx = MXU_DTYPE
    w_in_c = w_in.astype(mx)
    w_out_c = w_out.astype(mx)
    sgu_w_c = sgu_w.astype(mx)
    sgu_bias = jnp.repeat(jnp.swapaxes(sgu_b, 1, 2), GROUP_DIM, axis=2)
    wq_t = jnp.swapaxes(peer_wq, 1, 2).astype(mx)
    subkeys = peer_subkeys.reshape(depth, 2 * PEER_HEADS, PEER_NKEYS, PEER_DKEY // 2).astype(mx)
    n_blocks = peer_u.shape[1] // PEER_EXPERT_BLOCK
    u_c = peer_u.astype(mx).reshape(depth, n_blocks, PEER_EXPERT_BLOCK, d)
    vt_c = jnp.swapaxes(peer_v.astype(mx).reshape(depth, n_blocks, PEER_EXPERT_BLOCK, d), 2, 3)

    for l in range(depth):
        mod = mods[l]
        gb, z, ysgu, qkv = _proj(xa, mod, norm1_g[l][None], w_in_c[l], sgu_w_c[l], sgu_bias[l],
                                 sgu_norm_g[l][None], tiles_per_batch)
        yna = _attention(qkv, bias, l, batch, seq)
        xm, h2 = _merge(gb, z, ysgu, yna, xa, mod, conv_w[l], out_norm_g[l][None], w_out_c[l],
                        norm2_g[l][None], tiles_per_batch, seq, n_latent)
        xa = _peer(h2, xm, mod, wq_t, subkeys[l], u_c, vt_c, l, tiles_per_batch)
    return _final_norm(xa, final_norm_g[None], n_latent).reshape(batch, seq, d)
```

```python
import functools

import numpy as np
import jax
import jax.numpy as jnp
from jax import lax
from jax.experimental import pallas as pl
from jax.experimental.pallas import tpu as pltpu

D_MODEL = 1024
GRID_W = 64
CTX_LEN = 256
EPS = 1e-6
GROUP_DIM = 64
CONV_W = 256
SGU_W = 256
SGU_GROUPS = 4
CHUNK = 128
NA_HEADS = 8
NA_W = 512
NA_KH = 8
NA_KW = 16
MIX_COLS = 3 * CONV_W + 2 * SGU_W
IN_W = MIX_COLS + 3 * NA_W
PEER_HEADS = 8
PEER_NKEYS = 128
PEER_TOPK = 16
PEER_DKEY = 256

LANES = 128
SUBLANES = 8
TOKEN_TILE = 512
QUERY_ROWS = 4
QUERY_BLOCK = QUERY_ROWS * GRID_W
KEY_ROWS = 12
KEY_BLOCK = KEY_ROWS * GRID_W
MASKED = -1e30
PEER_EXPERT_BLOCK = 1024
PEER_PAIR = 2 * PEER_NKEYS
PEER_KEYS_PER_PASS = 2
PEER_OUT_CHUNKS = 4
VMEM_LIMIT = 60 * 1024 * 1024
PEER_SCHEDULER_FLAGS = None

MXU_DTYPE = jnp.bfloat16
GATE_DTYPE = jnp.bfloat16
F32 = jnp.float32
NT_DIMS = (((1,), (1,)), ((), ()))


def _cparams(sem, flags=None):
    return pltpu.CompilerParams(dimension_semantics=sem, vmem_limit_bytes=VMEM_LIMIT, flags=flags)


def _rms(x, g):
    ms = jnp.mean(x * x, axis=-1, keepdims=True)
    return x * lax.rsqrt(ms + EPS) * g


def _group_rms128(y, g, lo):
    y2 = y * y
    s_lo = jnp.sum(jnp.where(lo, y2, 0.0), axis=-1, keepdims=True)
    s_hi = jnp.sum(jnp.where(lo, 0.0, y2), axis=-1, keepdims=True)
    inv = 1.0 / GROUP_DIM
    r = jnp.where(lo, lax.rsqrt(s_lo * inv + EPS), lax.rsqrt(s_hi * inv + EPS))
    return y * r * g


def _lo_mask():
    return lax.broadcasted_iota(jnp.int32, (1, LANES), 1) < GROUP_DIM


def _mod_row(mod_ref, tiles_per_batch):
    g = jnp.minimum(pl.program_id(0) // tiles_per_batch, 2)
    return mod_ref[pl.ds(g, 1), :]


def _adaln_kernel(c_ref, w_ref, b_ref, o_ref):
    cv = c_ref[...]
    s = cv * (1.0 / (1.0 + jnp.exp(-cv)))
    o_ref[...] = jnp.dot(s.astype(MXU_DTYPE), w_ref[...].astype(MXU_DTYPE),
                         preferred_element_type=F32) + b_ref[...]


def _adaln(cvec, ada_w, ada_b):
    depth, d, cols = ada_w.shape
    cb = 1536
    return pl.pallas_call(
        _adaln_kernel,
        grid=(depth, cols // cb),
        in_specs=[pl.BlockSpec((SUBLANES, d), lambda l, j: (0, 0)),
                  pl.BlockSpec((None, d, cb), lambda l, j: (l, 0, j)),
                  pl.BlockSpec((None, 1, cb), lambda l, j: (l, 0, j))],
        out_specs=pl.BlockSpec((None, SUBLANES, cb), lambda l, j: (l, 0, j)),
        out_shape=jax.ShapeDtypeStruct((depth, SUBLANES, cols), F32),
        compiler_params=_cparams(("parallel", "parallel")),
    )(cvec, ada_w, ada_b.reshape(depth, 1, cols))


def _proj_kernel(x_ref, mod_ref, n1g_ref, w_ref, sw_ref, sb_ref, sg_ref,
                 gb_ref, z_ref, ysgu_ref, qkv_ref, *, tiles_per_batch):
    d = D_MODEL
    mrow = _mod_row(mod_ref, tiles_per_batch)
    sh1, sc1 = mrow[:, 0:d], mrow[:, d:2 * d]
    h = _rms(x_ref[...], n1g_ref[...]) * (1.0 + sc1) + sh1
    p = jnp.dot(h.astype(MXU_DTYPE), w_ref[...], preferred_element_type=F32)
    gb_ref[...] = p[:, 0:CONV_W]
    z_ref[...] = p[:, CONV_W:2 * CONV_W] * p[:, 2 * CONV_W:3 * CONV_W]
    sg = jax.nn.gelu(p[:, 3 * CONV_W:MIX_COLS])
    u, v = sg[:, :SGU_W], sg[:, SGU_W:]
    lo = _lo_mask()
    vn = jnp.concatenate(
        [_group_rms128(v[:, :LANES], sg_ref[:, :LANES], lo),
         _group_rms128(v[:, LANES:], sg_ref[:, LANES:], lo)], axis=1).astype(MXU_DTYPE)
    grp = lax.broadcasted_iota(jnp.int32, (1, SGU_W), 1) // GROUP_DIM
    for c in range(TOKEN_TILE // CHUNK):
        rows = slice(c * CHUNK, (c + 1) * CHUNK)
        vc = vn[rows, :]
        mixed = jnp.zeros((CHUNK, SGU_W), F32)
        for gi in range(SGU_GROUPS):
            mg = jnp.dot(sw_ref[gi], vc, preferred_element_type=F32)
            mixed = jnp.where(grp == gi, mg, mixed)
        ysgu_ref[rows, :] = u[rows, :] * (mixed + sb_ref[...])
    qkv_ref[...] = p[:, MIX_COLS:].astype(qkv_ref.dtype)


def _proj(xa, mod, n1g, w_in, sgu_w, sgu_bias, sgu_g, tiles_per_batch):
    n, d = xa.shape
    tm = TOKEN_TILE
    row = lambda w: pl.BlockSpec((tm, w), lambda t: (t, 0))
    full = lambda a: pl.BlockSpec(a.shape, lambda t: (0,) * a.ndim)
    return pl.pallas_call(
        functools.partial(_proj_kernel, tiles_per_batch=tiles_per_batch),
        grid=(n // tm,),
        in_specs=[row(d), full(mod), full(n1g), full(w_in), full(sgu_w), full(sgu_bias), full(sgu_g)],
        out_specs=[row(CONV_W), row(CONV_W), row(SGU_W), row(3 * NA_W)],
        out_shape=[jax.ShapeDtypeStruct((n, CONV_W), F32), jax.ShapeDtypeStruct((n, CONV_W), F32),
                   jax.ShapeDtypeStruct((n, SGU_W), F32), jax.ShapeDtypeStruct((n, 3 * NA_W), MXU_DTYPE)],
        compiler_params=_cparams(("parallel",)),
    )(xa, mod, n1g, w_in, sgu_w, sgu_bias, sgu_g)


def _attn_kernel(q_ref, k_ref, v_ref, kc_ref, vc_ref, bias_ref, o_ref, *, rows):
    j = pl.program_id(2)
    kr0 = jnp.clip(QUERY_ROWS * j - NA_KH // 2, 0, rows - KEY_ROWS)
    start = pl.multiple_of(kr0 * GRID_W, GRID_W)
    kl = k_ref[pl.ds(start, KEY_BLOCK), :]
    vl = v_ref[pl.ds(start, KEY_BLOCK), :]
    kc, vc, q = kc_ref[...], vc_ref[...], q_ref[...]
    lo = _lo_mask()
    scale = GROUP_DIM ** -0.5
    o = jnp.zeros((QUERY_BLOCK, LANES), F32)
    for hh in range(2):
        hm = lo if hh == 0 else jnp.logical_not(lo)
        qh = jnp.where(hm, q, jnp.zeros_like(q)) * scale
        s_loc = lax.dot_general(qh, kl, NT_DIMS, preferred_element_type=F32) + bias_ref[0, hh]
        s_ctx = lax.dot_general(qh, kc, NT_DIMS, preferred_element_type=F32)
        m = jnp.maximum(jnp.max(s_loc, axis=-1, keepdims=True), jnp.max(s_ctx, axis=-1, keepdims=True))
        p_loc = jnp.exp(s_loc - m)
        p_ctx = jnp.exp(s_ctx - m)
        den = jnp.sum(p_loc, axis=-1, keepdims=True) + jnp.sum(p_ctx, axis=-1, keepdims=True)
        vlh = jnp.where(hm, vl, jnp.zeros_like(vl))
        vch = jnp.where(hm, vc, jnp.zeros_like(vc))
        oh = (jnp.dot(p_loc.astype(MXU_DTYPE), vlh, preferred_element_type=F32)
              + jnp.dot(p_ctx.astype(MXU_DTYPE), vch, preferred_element_type=F32))
        o = o + oh * (1.0 / den)
    o_ref[...] = o


def _attention(qkv, bias, layer, batch, seq):
    n = qkv.shape[0]
    rows = seq // GRID_W
    nblk = rows // QUERY_ROWS
    ctx_q0 = batch * seq // QUERY_BLOCK
    heads_cols = NA_W // LANES

    def q_map(b, hp, j):
        return (jnp.where(j < nblk, b * nblk + j, ctx_q0 + b), hp)

    def variant(b, hp, j):
        v = jnp.where(j == 0, 0, jnp.where(j < nblk - 1, 1, jnp.where(j == nblk - 1, 2, 3)))
        return (layer, v, hp, 0, 0)

    return pl.pallas_call(
        functools.partial(_attn_kernel, rows=rows),
        grid=(batch, heads_cols, nblk + 1),
        in_specs=[pl.BlockSpec((QUERY_BLOCK, LANES), q_map),
                  pl.BlockSpec((seq, LANES), lambda b, hp, j: (b, heads_cols + hp)),
                  pl.BlockSpec((seq, LANES), lambda b, hp, j: (b, 2 * heads_cols + hp)),
                  pl.BlockSpec((CTX_LEN, LANES), lambda b, hp, j: (ctx_q0 + b, heads_cols + hp)),
                  pl.BlockSpec((CTX_LEN, LANES), lambda b, hp, j: (ctx_q0 + b, 2 * heads_cols + hp)),
                  pl.BlockSpec((None, 1, 2, QUERY_BLOCK, KEY_BLOCK), variant)],
        out_specs=pl.BlockSpec((QUERY_BLOCK, LANES), q_map),
        out_shape=jax.ShapeDtypeStruct((n, NA_W), F32),
        compiler_params=_cparams(("parallel", "parallel", "arbitrary")),
    )(qkv, qkv, qkv, qkv, qkv, bias)


def _bias_block_index(rows):
    kh = min(NA_KH, rows)
    idx = np.full((4, QUERY_ROWS, KEY_ROWS), 2 * NA_KH - 1, np.int32)
    for v, (r0, kr0) in enumerate(((0, 0), (QUERY_ROWS, 0), (rows - QUERY_ROWS, rows - KEY_ROWS))):
        for qr in range(QUERY_ROWS):
            r = r0 + qr
            rs = min(max(r - kh // 2, 0), rows - kh)
            for kr in range(KEY_ROWS):
                if rs <= kr0 + kr < rs + kh:
                    idx[v, qr, kr] = kr0 + kr - r + NA_KH - 1
    return idx


def _build_bias(rpb, block_index):
    depth, heads = rpb.shape[:2]
    pad = GRID_W - NA_KW
    padded = jnp.pad(rpb, ((0, 0), (0, 0), (0, 0), (pad, pad)), constant_values=MASKED)
    toe = jnp.stack([padded[..., GRID_W - 1 - qc:2 * GRID_W - 1 - qc] for qc in range(GRID_W)], axis=3)
    qc = np.arange(GRID_W)[:, None]
    kc = np.arange(GRID_W)[None, :]
    cs = np.clip(qc - NA_KW // 2, 0, GRID_W - NA_KW)
    toe = jnp.where((kc >= cs) & (kc < cs + NA_KW), toe, MASKED)
    toe = jnp.concatenate([toe, jnp.full((depth, heads, 1, GRID_W, GRID_W), MASKED, toe.dtype)], axis=2)
    flat = [int(a) for a in block_index.reshape(-1)]
    blocks = jnp.stack([toe[:, :, a] for a in flat], axis=2)
    blocks = blocks.reshape(depth, heads, 4, QUERY_ROWS, KEY_ROWS, GRID_W, GRID_W)
    return jnp.transpose(blocks, (0, 2, 1, 3, 5, 4, 6)).reshape(depth, 4, heads, QUERY_BLOCK, KEY_BLOCK)


def _merge_kernel(gb_ref, z_ref, zp_ref, zn_ref, ysgu_ref, yna_ref, x_ref, mod_ref, cw_ref, og_ref,
                  w_ref, n2g_ref, xo_ref, h2_ref, *, tiles_per_batch, seq, n_latent):
    d = D_MODEL
    tm = TOKEN_TILE
    t = pl.program_id(0)
    mrow = _mod_row(mod_ref, tiles_per_batch)
    g1, sh2, sc2 = mrow[:, 2 * d:3 * d], mrow[:, 3 * d:4 * d], mrow[:, 4 * d:5 * d]
    z = z_ref[...]
    ridx = lax.broadcasted_iota(jnp.int32, (tm, 1), 0)
    grow = t * tm + ridx
    seqlen = jnp.where(t * tm >= n_latent, CTX_LEN, seq)
    pos = grow & (seqlen - 1)
    z_prev = jnp.where(ridx == 0, zp_ref[SUBLANES - 1:SUBLANES, :], pltpu.roll(z, 1, 0))
    z_next = jnp.where(ridx == tm - 1, zn_ref[0:1, :], pltpu.roll(z, tm - 1, 0))
    z_prev = jnp.where(pos == 0, 0.0, z_prev)
    z_next = jnp.where(pos == seqlen - 1, 0.0, z_next)
    conv = z_prev * cw_ref[0:1, :] + z * cw_ref[1:2, :] + z_next * cw_ref[2:3, :]
    y_conv = gb_ref[...] * conv
    lo = _lo_mask()
    blocks = []
    for src, width in ((y_conv, CONV_W), (ysgu_ref[...], SGU_W), (yna_ref[...], NA_W)):
        for c in range(width // LANES):
            off = len(blocks) * LANES
            blocks.append(_group_rms128(src[:, c * LANES:(c + 1) * LANES], og_ref[:, off:off + LANES], lo)
                          .astype(MXU_DTYPE))
    y = jnp.dot(jnp.concatenate(blocks, axis=1), w_ref[...], preferred_element_type=F32)
    xm = x_ref[...] + g1 * y
    xo_ref[...] = xm
    h2_ref[...] = (_rms(xm, n2g_ref[...]) * (1.0 + sc2) + sh2).astype(h2_ref.dtype)


def _merge(gb, z, ysgu, yna, xa, mod, conv_w, out_g, w_out, n2g, tiles_per_batch, seq, n_latent):
    n, d = xa.shape
    tm = TOKEN_TILE
    hb = tm // SUBLANES
    nb8 = n // SUBLANES
    row = lambda w: pl.BlockSpec((tm, w), lambda t: (t, 0))
    full = lambda a: pl.BlockSpec(a.shape, lambda t: (0,) * a.ndim)
    halo_prev = pl.BlockSpec((SUBLANES, CONV_W), lambda t: (jnp.maximum(t * hb - 1, 0), 0))
    halo_next = pl.BlockSpec((SUBLANES, CONV_W), lambda t: (jnp.minimum((t + 1) * hb, nb8 - 1), 0))
    return pl.pallas_call(
        functools.partial(_merge_kernel, tiles_per_batch=tiles_per_batch, seq=seq, n_latent=n_latent),
        grid=(n // tm,),
        in_specs=[row(CONV_W), row(CONV_W), halo_prev, halo_next, row(SGU_W), row(NA_W), row(d),
                  full(mod), full(conv_w), full(out_g), full(w_out), full(n2g)],
        out_specs=[row(d), row(d)],
        out_shape=[jax.ShapeDtypeStruct((n, d), F32), jax.ShapeDtypeStruct((n, d), MXU_DTYPE)],
        compiler_params=_cparams(("parallel",)),
    )(gb, z, z, z, ysgu, yna, xa, mod, conv_w, out_g, w_out, n2g)


def _odd_even_merge(lo, hi, r):
    step = r * 2
    if step < hi - lo:
        yield from _odd_even_merge(lo, hi, step)
        yield from _odd_even_merge(lo + r, hi, step)
        for i in range(lo + r, hi - r, step):
            yield (i, i + r)
    else:
        yield (lo, lo + r)


def _odd_even_sort(lo, hi):
    if hi - lo >= 1:
        mid = lo + (hi - lo) // 2
        yield from _odd_even_sort(lo, mid)
        yield from _odd_even_sort(mid + 1, hi)
        yield from _odd_even_merge(lo, hi, 1)


def _sort_desc(vals):
    for i, j in _odd_even_sort(0, len(vals) - 1):
        vals[i], vals[j] = jnp.maximum(vals[i], vals[j]), jnp.minimum(vals[i], vals[j])
    return vals


def _merge_bitonic_desc(vals):
    n = len(vals)
    k = n // 2
    while k >= 1:
        for i in range(n):
            if (i & k) == 0:
                hi, lo = jnp.maximum(vals[i], vals[i + k]), jnp.minimum(vals[i], vals[i + k])
                vals[i], vals[i + k] = hi, lo
        k //= 2
    return vals


def _top_merge(a, b):
    n = len(a)
    return _merge_bitonic_desc([jnp.maximum(a[i], b[n - 1 - i]) for i in range(n)])


def _top16_over_keys(tiles):
    vals = _sort_desc(list(tiles))
    for shift in (4, 2, 1):
        vals = _top_merge(vals, [pltpu.roll(v, shift, 0) for v in vals])
    return vals


def _pair_threshold(a, b):
    k = PEER_TOPK
    rows = [[a[i] + b[j] for j in range(k // (i + 1))] for i in range(k)]
    neg = jnp.full(a[0].shape, -jnp.inf, F32)

    def merge(x, y, n):
        return _merge_bitonic_desc(x + [neg] * (n - len(x) - len(y)) + y[::-1])

    l1 = merge(rows[1], rows[2], k)
    l2 = merge(merge(rows[3], rows[4], 8), merge(merge(rows[5], rows[6], 4), rows[7], 8), k)
    l3 = [r[0] for r in rows[8:]] + [neg] * (k // 2)
    return _top_merge(_top_merge(rows[0], l1), _top_merge(l2, l3))


def _sorted_count(b, pred):
    one = lambda p, v: jnp.where(p, v, 0.0)
    p8 = pred(b[7])
    p4 = pred(jnp.where(p8, b[11], b[3]))
    p2 = pred(jnp.where(p8, jnp.where(p4, b[13], b[9]), jnp.where(p4, b[5], b[1])))
    lo = jnp.where(p4, jnp.where(p2, b[6], b[4]), jnp.where(p2, b[2], b[0]))
    hi = jnp.where(p4, jnp.where(p2, b[14], b[12]), jnp.where(p2, b[10], b[8]))
    p1 = pred(jnp.where(p8, hi, lo))
    p0 = pred(b[15])
    return one(p8, 8.0) + one(p4, 4.0) + one(p2, 2.0) + one(p1, 1.0) + one(p0, 1.0)


def _gelu_tanh(x):
    k1 = -2.0 * (2.0 / np.pi) ** 0.5
    k2 = k1 * 0.044715
    return x / (1.0 + jnp.exp(x * (k1 + k2 * (x * x))))


def _peer_kernel(zero_ref, h2_ref, x_ref, mod_ref, wq_ref, sk_ref, u_first_ref, u_odd_ref, u_next_ref, vt_ref,
                 o_ref, s_sc, cnt_sc, e0_sc, rk_sc, e1_sc, cnt_st, e0_st, h2t_sc, qt_sc,
                 act_e0, act_e1, act_e2, act_e3, act_o0, act_o1, act_o2, act_o3, w_e, w_o, acc_sc,
                 *, tiles_per_batch):
    d = D_MODEL
    tm = TOKEN_TILE
    nk = PEER_NKEYS
    key_tiles = nk // SUBLANES
    lane_tiles = tm // LANES
    packed = 2 * SUBLANES
    gate_tiles = nk // packed
    n_pairs = PEER_EXPERT_BLOCK // PEER_PAIR
    n_blocks = nk * nk // PEER_EXPERT_BLOCK
    e = pl.program_id(1)
    gdt = rk_sc.dtype
    assert PEER_EXPERT_BLOCK == SUBLANES * nk

    @pl.when(e == 0)
    def _scores_and_thresholds():
        h2t_sc[...] = h2_ref[...].astype(F32).T.astype(h2t_sc.dtype)
        for ip, buf in enumerate((act_e0, act_e1, act_e2, act_e3)):
            buf[0] = jnp.dot(u_first_ref[ip * PEER_PAIR:(ip + 1) * PEER_PAIR, :], h2t_sc[...],
                             preferred_element_type=F32)
        q_t = jnp.dot(wq_ref[...], h2t_sc[...], preferred_element_type=F32)
        qt_sc[...] = q_t.astype(qt_sc.dtype)
        for hp in range(2 * PEER_HEADS):
            s_t = jnp.dot(sk_ref[hp], qt_sc[hp * nk:(hp + 1) * nk, :], preferred_element_type=F32)
            for lt in range(lane_tiles):
                s_sc[hp, lt] = s_t[:, lt * LANES:(lt + 1) * LANES]
        acc_sc[...] = jnp.zeros_like(acc_sc)

        def head_tile(it, carry):
            h = it // lane_tiles
            lt = it % lane_tiles
            rows = lambda v: slice(v * SUBLANES, (v + 1) * SUBLANES)
            s0 = [s_sc[2 * h, lt, rows(v), :] for v in range(key_tiles)]
            s1 = [s_sc[2 * h + 1, lt, rows(v), :] for v in range(key_tiles)]
            a = _top16_over_keys(s0)
            b = _top16_over_keys(s1)
            best = _pair_threshold(a, b)
            tau = best[PEER_TOPK - 1]
            z = jnp.ones_like(tau)
            for kk in range(1, PEER_TOPK):
                z = z + jnp.exp(best[kk] - best[0])
            rz = 1.0 / z
            for v in range(key_tiles):
                cnt_sc[h, lt, rows(v), :] = _sorted_count(b, lambda bk: s0[v] + bk >= tau)
                e0_sc[h, lt, rows(v), :] = jnp.exp(s0[v] - a[0])
            for vp in range(gate_tiles):
                rk, e1 = [], []
                for v in (2 * vp, 2 * vp + 1):
                    rk.append(_sorted_count(b, lambda bk: bk > s1[v]))
                    e1.append(jnp.exp(s1[v] - b[0]) * rz)
                prow = slice(vp * packed, (vp + 1) * packed)
                rk_sc[h, lt, prow, :] = jnp.concatenate(rk, axis=0).astype(gdt)
                e1_sc[h, lt, prow, :] = jnp.concatenate(e1, axis=0).astype(gdt)
            return carry

        lax.fori_loop(0, PEER_HEADS * lane_tiles, head_tile, 0)

    base = pl.multiple_of(e * 2 * SUBLANES, 2 * SUBLANES)
    for h in range(PEER_HEADS):
        for lt in range(lane_tiles):
            cnt_st[h, lt] = cnt_sc[h, lt, pl.ds(base, 2 * SUBLANES), :]
            e0_st[h, lt] = e0_sc[h, lt, pl.ds(base, 2 * SUBLANES), :]
    z0 = zero_ref[0]

    def pre_activation(u_block, bufs, ip):
        bufs[ip][0] = jnp.dot(u_block[ip * PEER_PAIR:(ip + 1) * PEER_PAIR, :], h2t_sc[...],
                              preferred_element_type=F32)

    def gated_activation(s, bufs, w_buf, ip):
        kpp = PEER_KEYS_PER_PASS
        for lt in range(lane_tiles):
            lanes = slice(lt * LANES, (lt + 1) * LANES)
            for i0 in range(0, 2, kpp):
                gate = [[jnp.zeros((packed, LANES), gdt) for _ in range(gate_tiles)] for _ in range(kpp)]
                for h in range(PEER_HEADS):
                    def row(ref, ii):
                        r = s * SUBLANES + 2 * ip + i0 + ii
                        return jnp.broadcast_to(ref[h, lt, r:r + 1, :], (packed, LANES)).astype(gdt)
                    cnt = [row(cnt_st, ii) for ii in range(kpp)]
                    e0 = [row(e0_st, ii) for ii in range(kpp)]
                    for jp in range(gate_tiles):
                        prow = slice(jp * packed, (jp + 1) * packed)
                        rk = rk_sc[h, lt, prow, :]
                        e1 = e1_sc[h, lt, prow, :]
                        for ii in range(kpp):
                            gate[ii][jp] = gate[ii][jp] + jnp.where(rk < cnt[ii], e1 * e0[ii],
                                                                     jnp.zeros_like(e1))
                for ii in range(kpp):
                    for jp in range(gate_tiles):
                        r0 = (i0 + ii) * nk + jp * packed
                        act = _gelu_tanh(bufs[ip][z0, r0:r0 + packed, lanes].astype(gdt))
                        w0 = ip * PEER_PAIR + r0
                        w_buf[0, w0:w0 + packed, lanes] = (gate[ii][jp] * act).astype(w_buf.dtype)

    def output(s, w_buf):
        for ic in range(PEER_OUT_CHUNKS):
            rows = slice(ic * (d // PEER_OUT_CHUNKS), (ic + 1) * (d // PEER_OUT_CHUNKS))
            acc_sc[rows, :] += jnp.dot(vt_ref[s, rows, :], w_buf[z0], preferred_element_type=F32)

    even = (act_e0, act_e1, act_e2, act_e3)
    odd = (act_o0, act_o1, act_o2, act_o3)
    for ip in range(n_pairs):
        pre_activation(u_odd_ref, odd, ip)
        gated_activation(0, even, w_e, ip)
    output(0, w_e)
    for ip in range(n_pairs):
        pre_activation(u_next_ref, even, ip)
        gated_activation(1, odd, w_o, ip)
    output(1, w_o)

    @pl.when(e == pl.num_programs(1) - 1)
    def _residual():
        g2 = _mod_row(mod_ref, tiles_per_batch)[:, 5 * d:6 * d]
        o_ref[...] = x_ref[...] + g2 * acc_sc[...].T


def _peer(h2, xm, mod, wq_t, subkeys, u_tab, vt_tab, layer, tiles_per_batch):
    n, d = xm.shape
    tm = TOKEN_TILE
    eb = PEER_EXPERT_BLOCK
    nb = u_tab.shape[1]
    full = lambda a: pl.BlockSpec(a.shape, lambda t, e: (0,) * a.ndim)
    row = pl.BlockSpec((tm, d), lambda t, e: (t, 0))
    once = pl.BlockSpec((tm, d), lambda t, e: (t, 0), pipeline_mode=pl.Buffered(1))
    u_block = lambda blk: pl.BlockSpec((None, None, eb, d), lambda t, e: (layer, blk(e), 0, 0))
    hp2 = 2 * PEER_HEADS
    lt = tm // LANES
    return pl.pallas_call(
        functools.partial(_peer_kernel, tiles_per_batch=tiles_per_batch),
        grid=(n // tm, nb // 2),
        in_specs=[pl.BlockSpec(memory_space=pltpu.SMEM), once, once, full(mod),
                  pl.BlockSpec((None,) + wq_t.shape[1:], lambda t, e: (layer, 0, 0)), full(subkeys),
                  u_block(lambda e: 0),
                  u_block(lambda e: 2 * e + 1),
                  u_block(lambda e: jnp.minimum(2 * e + 2, nb - 1)),
                  pl.BlockSpec((None, 2, d, eb), lambda t, e: (layer, e, 0, 0))],
        out_specs=row,
        out_shape=jax.ShapeDtypeStruct((n, d), F32),
        scratch_shapes=[pltpu.VMEM((hp2, lt, PEER_NKEYS, LANES), F32),
                        pltpu.VMEM((PEER_HEADS, lt, PEER_NKEYS, LANES), F32),
                        pltpu.VMEM((PEER_HEADS, lt, PEER_NKEYS, LANES), F32),
                        pltpu.VMEM((PEER_HEADS, lt, PEER_NKEYS, LANES), GATE_DTYPE),
                        pltpu.VMEM((PEER_HEADS, lt, PEER_NKEYS, LANES), GATE_DTYPE),
                        pltpu.VMEM((PEER_HEADS, lt, 2 * SUBLANES, LANES), F32),
                        pltpu.VMEM((PEER_HEADS, lt, 2 * SUBLANES, LANES), F32),
                        pltpu.VMEM((d, tm), MXU_DTYPE),
                        pltpu.VMEM((hp2 * PEER_NKEYS, tm), MXU_DTYPE)]
                       + [pltpu.VMEM((1, PEER_PAIR, tm), F32)] * 8
                       + [pltpu.VMEM((1, eb, tm), MXU_DTYPE)] * 2
                       + [pltpu.VMEM((d, tm), F32)],
        compiler_params=_cparams(("parallel", "arbitrary"), PEER_SCHEDULER_FLAGS),
    )(jnp.zeros((1,), jnp.int32), h2, xm, mod, wq_t, subkeys, u_tab, u_tab, u_tab, vt_tab)


def _final_kernel(x_ref, g_ref, o_ref):
    o_ref[...] = _rms(x_ref[...], g_ref[...])


def _final_norm(xa, g, n_latent):
    d = xa.shape[1]
    tm = TOKEN_TILE
    return pl.pallas_call(
        _final_kernel,
        grid=(n_latent // tm,),
        in_specs=[pl.BlockSpec((tm, d), lambda t: (t, 0)), pl.BlockSpec((1, d), lambda t: (0, 0))],
        out_specs=pl.BlockSpec((tm, d), lambda t: (t, 0)),
        out_shape=jax.ShapeDtypeStruct((n_latent, d), F32),
        compiler_params=_cparams(("parallel",)),
    )(xa, g)


def kernel(x, c, ctx, c_ctx, ada_w, ada_b, norm1_g, w_in, conv_w, sgu_w, sgu_b, sgu_norm_g, na_rpb,
           out_norm_g, w_out, norm2_g, peer_wq, peer_subkeys, peer_u, peer_v, final_norm_g):
    batch, seq, d = x.shape
    depth = ada_w.shape[0]
    assert d == D_MODEL and batch == 2 and ctx.shape[1] == CTX_LEN
    assert seq % TOKEN_TILE == 0 and batch * CTX_LEN == TOKEN_TILE
    rows = seq // GRID_W
    assert rows % QUERY_ROWS == 0 and rows >= KEY_ROWS + QUERY_ROWS
    n_latent = batch * seq
    tiles_per_batch = seq // TOKEN_TILE

    xa = jnp.concatenate([x.reshape(n_latent, d), ctx.reshape(batch * CTX_LEN, d)], axis=0)
    cvec = jnp.zeros((SUBLANES, d), F32).at[0:batch].set(c).at[batch].set(c_ctx)
    mods = _adaln(cvec, ada_w, ada_b)
    bias = _build_bias(na_rpb, _bias_block_index(rows))

    mx = MXU_DTYPE
    w_in_c = w_in.astype(mx)
    w_out_c = w_out.astype(mx)
    sgu_w_c = sgu_w.astype(mx)
    sgu_bias = jnp.repeat(jnp.swapaxes(sgu_b, 1, 2), GROUP_DIM, axis=2)
    wq_t = jnp.swapaxes(peer_wq, 1, 2).astype(mx)
    subkeys = peer_subkeys.reshape(depth, 2 * PEER_HEADS, PEER_NKEYS, PEER_DKEY // 2).astype(mx)
    n_blocks = peer_u.shape[1] // PEER_EXPERT_BLOCK
    u_c = peer_u.astype(mx).reshape(depth, n_blocks, PEER_EXPERT_BLOCK, d)
    vt_c = jnp.swapaxes(peer_v.astype(mx).reshape(depth, n_blocks, PEER_EXPERT_BLOCK, d), 2, 3)

    for l in range(depth):
        mod = mods[l]
        gb, z, ysgu, qkv = _proj(xa, mod, norm1_g[l][None], w_in_c[l], sgu_w_c[l], sgu_bias[l],
                                 sgu_norm_g[l][None], tiles_per_batch)
        yna = _attention(qkv, bias, l, batch, seq)
        xm, h2 = _merge(gb, z, ysgu, yna, xa, mod, conv_w[l], out_norm_g[l][None], w_out_c[l],
                        norm2_g[l][None], tiles_per_batch, seq, n_latent)
        xa = _peer(h2, xm, mod, wq_t, subkeys[l], u_c, vt_c, l, tiles_per_batch)
    return _final_norm(xa, final_norm_g[None], n_latent).reshape(batch, seq, d)
```

```python
import functools

import numpy as np
import jax
import jax.numpy as jnp
from jax import lax
from jax.experimental import pallas as pl
from jax.experimental.pallas import tpu as pltpu

D_MODEL = 1024
GRID_W = 64
CTX_LEN = 256
EPS = 1e-6
GROUP_DIM = 64
CONV_W = 256
SGU_W = 256
SGU_GROUPS = 4
CHUNK = 128
NA_HEADS = 8
NA_W = 512
NA_KH = 8
NA_KW = 16
MIX_COLS = 3 * CONV_W + 2 * SGU_W
IN_W = MIX_COLS + 3 * NA_W
PEER_HEADS = 8
PEER_NKEYS = 128
PEER_TOPK = 16
PEER_DKEY = 256

LANES = 128
SUBLANES = 8
TOKEN_TILE = 512
QUERY_ROWS = 4
QUERY_BLOCK = QUERY_ROWS * GRID_W
KEY_ROWS = 12
KEY_BLOCK = KEY_ROWS * GRID_W
MASKED = -1e30
PEER_EXPERT_BLOCK = 1024
PEER_PAIR = 2 * PEER_NKEYS
PEER_KEYS_PER_PASS = 2
PEER_OUT_CHUNKS = 4
VMEM_LIMIT = 60 * 1024 * 1024
PEER_SCHEDULER_FLAGS = None

MXU_DTYPE = jnp.bfloat16
GATE_DTYPE = jnp.bfloat16
F32 = jnp.float32
NT_DIMS = (((1,), (1,)), ((), ()))


def _cparams(sem, flags=None):
    return pltpu.CompilerParams(dimension_semantics=sem, vmem_limit_bytes=VMEM_LIMIT, flags=flags)


def _rms(x, g):
    ms = jnp.mean(x * x, axis=-1, keepdims=True)
    return x * lax.rsqrt(ms + EPS) * g


def _group_rms128(y, g, lo):
    y2 = y * y
    s_lo = jnp.sum(jnp.where(lo, y2, 0.0), axis=-1, keepdims=True)
    s_hi = jnp.sum(jnp.where(lo, 0.0, y2), axis=-1, keepdims=True)
    inv = 1.0 / GROUP_DIM
    r = jnp.where(lo, lax.rsqrt(s_lo * inv + EPS), lax.rsqrt(s_hi * inv + EPS))
    return y * r * g


def _lo_mask():
    return lax.broadcasted_iota(jnp.int32, (1, LANES), 1) < GROUP_DIM


def _mod_row(mod_ref, tiles_per_batch):
    g = jnp.minimum(pl.program_id(0) // tiles_per_batch, 2)
    return mod_ref[pl.ds(g, 1), :]


def _adaln_kernel(c_ref, w_ref, b_ref, o_ref):
    cv = c_ref[...]
    s = cv * (1.0 / (1.0 + jnp.exp(-cv)))
    o_ref[...] = jnp.dot(s.astype(MXU_DTYPE), w_ref[...].astype(MXU_DTYPE),
                         preferred_element_type=F32) + b_ref[...]


def _adaln(cvec, ada_w, ada_b):
    depth, d, cols = ada_w.shape
    cb = 1536
    return pl.pallas_call(
        _adaln_kernel,
        grid=(depth, cols // cb),
        in_specs=[pl.BlockSpec((SUBLANES, d), lambda l, j: (0, 0)),
                  pl.BlockSpec((None, d, cb), lambda l, j: (l, 0, j)),
                  pl.BlockSpec((None, 1, cb), lambda l, j: (l, 0, j))],
        out_specs=pl.BlockSpec((None, SUBLANES, cb), lambda l, j: (l, 0, j)),
        out_shape=jax.ShapeDtypeStruct((depth, SUBLANES, cols), F32),
        compiler_params=_cparams(("parallel", "parallel")),
    )(cvec, ada_w, ada_b.reshape(depth, 1, cols))


def _proj_kernel(x_ref, mod_ref, n1g_ref, w_ref, sw_ref, sb_ref, sg_ref,
                 gb_ref, z_ref, ysgu_ref, qkv_ref, *, tiles_per_batch):
    d = D_MODEL
    mrow = _mod_row(mod_ref, tiles_per_batch)
    sh1, sc1 = mrow[:, 0:d], mrow[:, d:2 * d]
    h = _rms(x_ref[...], n1g_ref[...]) * (1.0 + sc1) + sh1
    p = jnp.dot(h.astype(MXU_DTYPE), w_ref[...], preferred_element_type=F32)
    gb_ref[...] = p[:, 0:CONV_W]
    z_ref[...] = p[:, CONV_W:2 * CONV_W] * p[:, 2 * CONV_W:3 * CONV_W]
    sg = jax.nn.gelu(p[:, 3 * CONV_W:MIX_COLS])
    u, v = sg[:, :SGU_W], sg[:, SGU_W:]
    lo = _lo_mask()
    vn = jnp.concatenate(
        [_group_rms128(v[:, :LANES], sg_ref[:, :LANES], lo),
         _group_rms128(v[:, LANES:], sg_ref[:, LANES:], lo)], axis=1).astype(MXU_DTYPE)
    grp = lax.broadcasted_iota(jnp.int32, (1, SGU_W), 1) // GROUP_DIM
    for c in range(TOKEN_TILE // CHUNK):
        rows = slice(c * CHUNK, (c + 1) * CHUNK)
        vc = vn[rows, :]
        mixed = jnp.zeros((CHUNK, SGU_W), F32)
        for gi in range(SGU_GROUPS):
            mg = jnp.dot(sw_ref[gi], vc, preferred_element_type=F32)
            mixed = jnp.where(grp == gi, mg, mixed)
        ysgu_ref[rows, :] = u[rows, :] * (mixed + sb_ref[...])
    qkv_ref[...] = p[:, MIX_COLS:].astype(qkv_ref.dtype)


def _proj(xa, mod, n1g, w_in, sgu_w, sgu_bias, sgu_g, tiles_per_batch):
    n, d = xa.shape
    tm = TOKEN_TILE
    row = lambda w: pl.BlockSpec((tm, w), lambda t: (t, 0))
    full = lambda a: pl.BlockSpec(a.shape, lambda t: (0,) * a.ndim)
    return pl.pallas_call(
        functools.partial(_proj_kernel, tiles_per_batch=tiles_per_batch),
        grid=(n // tm,),
        in_specs=[row(d), full(mod), full(n1g), full(w_in), full(sgu_w), full(sgu_bias), full(sgu_g)],
        out_specs=[row(CONV_W), row(CONV_W), row(SGU_W), row(3 * NA_W)],
        out_shape=[jax.ShapeDtypeStruct((n, CONV_W), F32), jax.ShapeDtypeStruct((n, CONV_W), F32),
                   jax.ShapeDtypeStruct((n, SGU_W), F32), jax.ShapeDtypeStruct((n, 3 * NA_W), MXU_DTYPE)],
        compiler_params=_cparams(("parallel",)),
    )(xa, mod, n1g, w_in, sgu_w, sgu_bias, sgu_g)


def _attn_kernel(q_ref, k_ref, v_ref, kc_ref, vc_ref, bias_ref, o_ref, *, rows):
    j = pl.program_id(2)
    kr0 = jnp.clip(QUERY_ROWS * j - NA_KH // 2, 0, rows - KEY_ROWS)
    start = pl.multiple_of(kr0 * GRID_W, GRID_W)
    kl = k_ref[pl.ds(start, KEY_BLOCK), :]
    vl = v_ref[pl.ds(start, KEY_BLOCK), :]
    kc, vc, q = kc_ref[...], vc_ref[...], q_ref[...]
    lo = _lo_mask()
    scale = GROUP_DIM ** -0.5
    o = jnp.zeros((QUERY_BLOCK, LANES), F32)
    for hh in range(2):
        hm = lo if hh == 0 else jnp.logical_not(lo)
        qh = jnp.where(hm, q, jnp.zeros_like(q)) * scale
        s_loc = lax.dot_general(qh, kl, NT_DIMS, preferred_element_type=F32) + bias_ref[0, hh]
        s_ctx = lax.dot_general(qh, kc, NT_DIMS, preferred_element_type=F32)
        m = jnp.maximum(jnp.max(s_loc, axis=-1, keepdims=True), jnp.max(s_ctx, axis=-1, keepdims=True))
        p_loc = jnp.exp(s_loc - m)
        p_ctx = jnp.exp(s_ctx - m)
        den = jnp.sum(p_loc, axis=-1, keepdims=True) + jnp.sum(p_ctx, axis=-1, keepdims=True)
        vlh = jnp.where(hm, vl, jnp.zeros_like(vl))
        vch = jnp.where(hm, vc, jnp.zeros_like(vc))
        oh = (jnp.dot(p_loc.astype(MXU_DTYPE), vlh, preferred_element_type=F32)
              + jnp.dot(p_ctx.astype(MXU_DTYPE), vch, preferred_element_type=F32))
        o = o + oh * (1.0 / den)
    o_ref[...] = o


def _attention(qkv, bias, layer, batch, seq):
    n = qkv.shape[0]
    rows = seq // GRID_W
    nblk = rows // QUERY_ROWS
    ctx_q0 = batch * seq // QUERY_BLOCK
    heads_cols = NA_W // LANES

    def q_map(b, hp, j):
        return (jnp.where(j < nblk, b * nblk + j, ctx_q0 + b), hp)

    def variant(b, hp, j):
        v = jnp.where(j == 0, 0, jnp.where(j < nblk - 1, 1, jnp.where(j == nblk - 1, 2, 3)))
        return (layer, v, hp, 0, 0)

    return pl.pallas_call(
        functools.partial(_attn_kernel, rows=rows),
        grid=(batch, heads_cols, nblk + 1),
        in_specs=[pl.BlockSpec((QUERY_BLOCK, LANES), q_map),
                  pl.BlockSpec((seq, LANES), lambda b, hp, j: (b, heads_cols + hp)),
                  pl.BlockSpec((seq, LANES), lambda b, hp, j: (b, 2 * heads_cols + hp)),
                  pl.BlockSpec((CTX_LEN, LANES), lambda b, hp, j: (ctx_q0 + b, heads_cols + hp)),
                  pl.BlockSpec((CTX_LEN, LANES), lambda b, hp, j: (ctx_q0 + b, 2 * heads_cols + hp)),
                  pl.BlockSpec((None, 1, 2, QUERY_BLOCK, KEY_BLOCK), variant)],
        out_specs=pl.BlockSpec((QUERY_BLOCK, LANES), q_map),
        out_shape=jax.ShapeDtypeStruct((n, NA_W), F32),
        compiler_params=_cparams(("parallel", "parallel", "arbitrary")),
    )(qkv, qkv, qkv, qkv, qkv, bias)


def _bias_block_index(rows):
    kh = min(NA_KH, rows)
    idx = np.full((4, QUERY_ROWS, KEY_ROWS), 2 * NA_KH - 1, np.int32)
    for v, (r0, kr0) in enumerate(((0, 0), (QUERY_ROWS, 0), (rows - QUERY_ROWS, rows - KEY_ROWS))):
        for qr in range(QUERY_ROWS):
            r = r0 + qr
            rs = min(max(r - kh // 2, 0), rows - kh)
            for kr in range(KEY_ROWS):
                if rs <= kr0 + kr < rs + kh:
                    idx[v, qr, kr] = kr0 + kr - r + NA_KH - 1
    return idx


def _build_bias(rpb, block_index):
    depth, heads = rpb.shape[:2]
    pad = GRID_W - NA_KW
    padded = jnp.pad(rpb, ((0, 0), (0, 0), (0, 0), (pad, pad)), constant_values=MASKED)
    toe = jnp.stack([padded[..., GRID_W - 1 - qc:2 * GRID_W - 1 - qc] for qc in range(GRID_W)], axis=3)
    qc = np.arange(GRID_W)[:, None]
    kc = np.arange(GRID_W)[None, :]
    cs = np.clip(qc - NA_KW // 2, 0, GRID_W - NA_KW)
    toe = jnp.where((kc >= cs) & (kc < cs + NA_KW), toe, MASKED)
    toe = jnp.concatenate([toe, jnp.full((depth, heads, 1, GRID_W, GRID_W), MASKED, toe.dtype)], axis=2)
    flat = [int(a) for a in block_index.reshape(-1)]
    blocks = jnp.stack([toe[:, :, a] for a in flat], axis=2)
    blocks = blocks.reshape(depth, heads, 4, QUERY_ROWS, KEY_ROWS, GRID_W, GRID_W)
    return jnp.transpose(blocks, (0, 2, 1, 3, 5, 4, 6)).reshape(depth, 4, heads, QUERY_BLOCK, KEY_BLOCK)


def _merge_kernel(gb_ref, z_ref, zp_ref, zn_ref, ysgu_ref, yna_ref, x_ref, mod_ref, cw_ref, og_ref,
                  w_ref, n2g_ref, xo_ref, h2_ref, *, tiles_per_batch, seq, n_latent):
    d = D_MODEL
    tm = TOKEN_TILE
    t = pl.program_id(0)
    mrow = _mod_row(mod_ref, tiles_per_batch)
    g1, sh2, sc2 = mrow[:, 2 * d:3 * d], mrow[:, 3 * d:4 * d], mrow[:, 4 * d:5 * d]
    z = z_ref[...]
    ridx = lax.broadcasted_iota(jnp.int32, (tm, 1), 0)
    grow = t * tm + ridx
    seqlen = jnp.where(t * tm >= n_latent, CTX_LEN, seq)
    pos = grow & (seqlen - 1)
    z_prev = jnp.where(ridx == 0, zp_ref[SUBLANES - 1:SUBLANES, :], pltpu.roll(z, 1, 0))
    z_next = jnp.where(ridx == tm - 1, zn_ref[0:1, :], pltpu.roll(z, tm - 1, 0))
    z_prev = jnp.where(pos == 0, 0.0, z_prev)
    z_next = jnp.where(pos == seqlen - 1, 0.0, z_next)
    conv = z_prev * cw_ref[0:1, :] + z * cw_ref[1:2, :] + z_next * cw_ref[2:3, :]
    y_conv = gb_ref[...] * conv
    lo = _lo_mask()
    blocks = []
    for src, width in ((y_conv, CONV_W), (ysgu_ref[...], SGU_W), (yna_ref[...], NA_W)):
        for c in range(width // LANES):
            off = len(blocks) * LANES
            blocks.append(_group_rms128(src[:, c * LANES:(c + 1) * LANES], og_ref[:, off:off + LANES], lo)
                          .astype(MXU_DTYPE))
    y = jnp.dot(jnp.concatenate(blocks, axis=1), w_ref[...], preferred_element_type=F32)
    xm = x_ref[...] + g1 * y
    xo_ref[...] = xm
    h2_ref[...] = (_rms(xm, n2g_ref[...]) * (1.0 + sc2) + sh2).astype(h2_ref.dtype)


def _merge(gb, z, ysgu, yna, xa, mod, conv_w, out_g, w_out, n2g, tiles_per_batch, seq, n_latent):
    n, d = xa.shape
    tm = TOKEN_TILE
    hb = tm // SUBLANES
    nb8 = n // SUBLANES
    row = lambda w: pl.BlockSpec((tm, w), lambda t: (t, 0))
    full = lambda a: pl.BlockSpec(a.shape, lambda t: (0,) * a.ndim)
    halo_prev = pl.BlockSpec((SUBLANES, CONV_W), lambda t: (jnp.maximum(t * hb - 1, 0), 0))
    halo_next = pl.BlockSpec((SUBLANES, CONV_W), lambda t: (jnp.minimum((t + 1) * hb, nb8 - 1), 0))
    return pl.pallas_call(
        functools.partial(_merge_kernel, tiles_per_batch=tiles_per_batch, seq=seq, n_latent=n_latent),
        grid=(n // tm,),
        in_specs=[row(CONV_W), row(CONV_W), halo_prev, halo_next, row(SGU_W), row(NA_W), row(d),
                  full(mod), full(conv_w), full(out_g), full(w_out), full(n2g)],
        out_specs=[row(d), row(d)],
        out_shape=[jax.ShapeDtypeStruct((n, d), F32), jax.ShapeDtypeStruct((n, d), MXU_DTYPE)],
        compiler_params=_cparams(("parallel",)),
    )(gb, z, z, z, ysgu, yna, xa, mod, conv_w, out_g, w_out, n2g)


def _odd_even_merge(lo, hi, r):
    step = r * 2
    if step < hi - lo:
        yield from _odd_even_merge(lo, hi, step)
        yield from _odd_even_merge(lo + r, hi, step)
        for i in range(lo + r, hi - r, step):
            yield (i, i + r)
    else:
        yield (lo, lo + r)


def _odd_even_sort(lo, hi):
    if hi - lo >= 1:
        mid = lo + (hi - lo) // 2
        yield from _odd_even_sort(lo, mid)
        yield from _odd_even_sort(mid + 1, hi)
        yield from _odd_even_merge(lo, hi, 1)


def _sort_desc(vals):
    for i, j in _odd_even_sort(0, len(vals) - 1):
        vals[i], vals[j] = jnp.maximum(vals[i], vals[j]), jnp.minimum(vals[i], vals[j])
    return vals


def _merge_bitonic_desc(vals):
    n = len(vals)
    k = n // 2
    while k >= 1:
        for i in range(n):
            if (i & k) == 0:
                hi, lo = jnp.maximum(vals[i], vals[i + k]), jnp.minimum(vals[i], vals[i + k])
                vals[i], vals[i + k] = hi, lo
        k //= 2
    return vals


def _top_merge(a, b):
    n = len(a)
    return _merge_bitonic_desc([jnp.maximum(a[i], b[n - 1 - i]) for i in range(n)])


def _top16_over_keys(tiles):
    vals = _sort_desc(list(tiles))
    for shift in (4, 2, 1):
        vals = _top_merge(vals, [pltpu.roll(v, shift, 0) for v in vals])
    return vals


def _pair_threshold(a, b):
    k = PEER_TOPK
    rows = [[a[i] + b[j] for j in range(k // (i + 1))] for i in range(k)]
    neg = jnp.full(a[0].shape, -jnp.inf, F32)

    def merge(x, y, n):
        return _merge_bitonic_desc(x + [neg] * (n - len(x) - len(y)) + y[::-1])

    l1 = merge(rows[1], rows[2], k)
    l2 = merge(merge(rows[3], rows[4], 8), merge(merge(rows[5], rows[6], 4), rows[7], 8), k)
    l3 = [r[0] for r in rows[8:]] + [neg] * (k // 2)
    return _top_merge(_top_merge(rows[0], l1), _top_merge(l2, l3))


def _sorted_count(b, pred):
    one = lambda p, v: jnp.where(p, v, 0.0)
    p8 = pred(b[7])
    p4 = pred(jnp.where(p8, b[11], b[3]))
    p2 = pred(jnp.where(p8, jnp.where(p4, b[13], b[9]), jnp.where(p4, b[5], b[1])))
    lo = jnp.where(p4, jnp.where(p2, b[6], b[4]), jnp.where(p2, b[2], b[0]))
    hi = jnp.where(p4, jnp.where(p2, b[14], b[12]), jnp.where(p2, b[10], b[8]))
    p1 = pred(jnp.where(p8, hi, lo))
    p0 = pred(b[15])
    return one(p8, 8.0) + one(p4, 4.0) + one(p2, 2.0) + one(p1, 1.0) + one(p0, 1.0)


def _gelu_tanh(x):
    k1 = -2.0 * (2.0 / np.pi) ** 0.5
    k2 = k1 * 0.044715
    return x / (1.0 + jnp.exp(x * (k1 + k2 * (x * x))))


def _peer_kernel(zero_ref, h2_ref, x_ref, mod_ref, fg_ref, wq_ref, sk_ref, u_first_ref, u_odd_ref, u_next_ref,
                 vt_ref, o_ref, s_sc, cnt_sc, e0_sc, rk_sc, e1_sc, cnt_st, e0_st, h2t_sc, qt_sc,
                 act_e0, act_e1, act_e2, act_e3, act_o0, act_o1, act_o2, act_o3, w_e, w_o, acc_sc,
                 *, tiles_per_batch, final_norm):
    d = D_MODEL
    tm = TOKEN_TILE
    nk = PEER_NKEYS
    key_tiles = nk // SUBLANES
    lane_tiles = tm // LANES
    packed = 2 * SUBLANES
    gate_tiles = nk // packed
    n_pairs = PEER_EXPERT_BLOCK // PEER_PAIR
    n_blocks = nk * nk // PEER_EXPERT_BLOCK
    e = pl.program_id(1)
    gdt = rk_sc.dtype
    assert PEER_EXPERT_BLOCK == SUBLANES * nk

    @pl.when(e == 0)
    def _scores_and_thresholds():
        h2t_sc[...] = h2_ref[...].astype(F32).T.astype(h2t_sc.dtype)
        for ip, buf in enumerate((act_e0, act_e1, act_e2, act_e3)):
            buf[0] = jnp.dot(u_first_ref[ip * PEER_PAIR:(ip + 1) * PEER_PAIR, :], h2t_sc[...],
                             preferred_element_type=F32)
        q_t = jnp.dot(wq_ref[...], h2t_sc[...], preferred_element_type=F32)
        qt_sc[...] = q_t.astype(qt_sc.dtype)
        for hp in range(2 * PEER_HEADS):
            s_t = jnp.dot(sk_ref[hp], qt_sc[hp * nk:(hp + 1) * nk, :], preferred_element_type=F32)
            for lt in range(lane_tiles):
                s_sc[hp, lt] = s_t[:, lt * LANES:(lt + 1) * LANES]
        acc_sc[...] = jnp.zeros_like(acc_sc)

        def head_tile(it, carry):
            h = it // lane_tiles
            lt = it % lane_tiles
            rows = lambda v: slice(v * SUBLANES, (v + 1) * SUBLANES)
            s0 = [s_sc[2 * h, lt, rows(v), :] for v in range(key_tiles)]
            s1 = [s_sc[2 * h + 1, lt, rows(v), :] for v in range(key_tiles)]
            a = _top16_over_keys(s0)
            b = _top16_over_keys(s1)
            best = _pair_threshold(a, b)
            tau = best[PEER_TOPK - 1]
            z = jnp.ones_like(tau)
            for kk in range(1, PEER_TOPK):
                z = z + jnp.exp(best[kk] - best[0])
            rz = 1.0 / z
            for v in range(key_tiles):
                cnt_sc[h, lt, rows(v), :] = _sorted_count(b, lambda bk: s0[v] + bk >= tau)
                e0_sc[h, lt, rows(v), :] = jnp.exp(s0[v] - a[0])
            for vp in range(gate_tiles):
                rk, e1 = [], []
                for v in (2 * vp, 2 * vp + 1):
                    rk.append(_sorted_count(b, lambda bk: bk > s1[v]))
                    e1.append(jnp.exp(s1[v] - b[0]) * rz)
                prow = slice(vp * packed, (vp + 1) * packed)
                rk_sc[h, lt, prow, :] = jnp.concatenate(rk, axis=0).astype(gdt)
                e1_sc[h, lt, prow, :] = jnp.concatenate(e1, axis=0).astype(gdt)
            return carry

        lax.fori_loop(0, PEER_HEADS * lane_tiles, head_tile, 0)

    base = pl.multiple_of(e * 2 * SUBLANES, 2 * SUBLANES)
    for h in range(PEER_HEADS):
        for lt in range(lane_tiles):
            cnt_st[h, lt] = cnt_sc[h, lt, pl.ds(base, 2 * SUBLANES), :]
            e0_st[h, lt] = e0_sc[h, lt, pl.ds(base, 2 * SUBLANES), :]
    z0 = zero_ref[0]

    def pre_activation(u_block, bufs, ip):
        bufs[ip][0] = jnp.dot(u_block[ip * PEER_PAIR:(ip + 1) * PEER_PAIR, :], h2t_sc[...],
                              preferred_element_type=F32)

    def gated_activation(s, bufs, w_buf, ip):
        kpp = PEER_KEYS_PER_PASS
        for lt in range(lane_tiles):
            lanes = slice(lt * LANES, (lt + 1) * LANES)
            for i0 in range(0, 2, kpp):
                gate = [[jnp.zeros((packed, LANES), gdt) for _ in range(gate_tiles)] for _ in range(kpp)]
                for h in range(PEER_HEADS):
                    def row(ref, ii):
                        r = s * SUBLANES + 2 * ip + i0 + ii
                        return jnp.broadcast_to(ref[h, lt, r:r + 1, :], (packed, LANES)).astype(gdt)
                    cnt = [row(cnt_st, ii) for ii in range(kpp)]
                    e0 = [row(e0_st, ii) for ii in range(kpp)]
                    for jp in range(gate_tiles):
                        prow = slice(jp * packed, (jp + 1) * packed)
                        rk = rk_sc[h, lt, prow, :]
                        e1 = e1_sc[h, lt, prow, :]
                        for ii in range(kpp):
                            gate[ii][jp] = gate[ii][jp] + jnp.where(rk < cnt[ii], e1 * e0[ii],
                                                                     jnp.zeros_like(e1))
                for ii in range(kpp):
                    for jp in range(gate_tiles):
                        r0 = (i0 + ii) * nk + jp * packed
                        act = _gelu_tanh(bufs[ip][z0, r0:r0 + packed, lanes].astype(gdt))
                        w0 = ip * PEER_PAIR + r0
                        w_buf[0, w0:w0 + packed, lanes] = (gate[ii][jp] * act).astype(w_buf.dtype)

    def output(s, w_buf):
        for ic in range(PEER_OUT_CHUNKS):
            rows = slice(ic * (d // PEER_OUT_CHUNKS), (ic + 1) * (d // PEER_OUT_CHUNKS))
            acc_sc[rows, :] += jnp.dot(vt_ref[s, rows, :], w_buf[z0], preferred_element_type=F32)

    even = (act_e0, act_e1, act_e2, act_e3)
    odd = (act_o0, act_o1, act_o2, act_o3)
    for ip in range(n_pairs):
        pre_activation(u_odd_ref, odd, ip)
        gated_activation(0, even, w_e, ip)
    output(0, w_e)
    for ip in range(n_pairs):
        pre_activation(u_next_ref, even, ip)
        gated_activation(1, odd, w_o, ip)
    output(1, w_o)

    @pl.when(e == pl.num_programs(1) - 1)
    def _residual():
        g2 = _mod_row(mod_ref, tiles_per_batch)[:, 5 * d:6 * d]
        y = x_ref[...] + g2 * acc_sc[...].T
        o_ref[...] = _rms(y, fg_ref[...]) if final_norm else y


def _peer(h2, xm, mod, final_g, wq_t, subkeys, u_tab, vt_tab, layer, tiles_per_batch, n, final_norm):
    d = xm.shape[1]
    tm = TOKEN_TILE
    eb = PEER_EXPERT_BLOCK
    nb = u_tab.shape[1]
    full = lambda a: pl.BlockSpec(a.shape, lambda t, e: (0,) * a.ndim)
    row = pl.BlockSpec((tm, d), lambda t, e: (t, 0))
    once = pl.BlockSpec((tm, d), lambda t, e: (t, 0), pipeline_mode=pl.Buffered(1))
    u_block = lambda blk: pl.BlockSpec((None, None, eb, d), lambda t, e: (layer, blk(e), 0, 0))
    hp2 = 2 * PEER_HEADS
    lt = tm // LANES
    return pl.pallas_call(
        functools.partial(_peer_kernel, tiles_per_batch=tiles_per_batch, final_norm=final_norm),
        grid=(n // tm, nb // 2),
        in_specs=[pl.BlockSpec(memory_space=pltpu.SMEM), once, once, full(mod), full(final_g),
                  pl.BlockSpec((None,) + wq_t.shape[1:], lambda t, e: (layer, 0, 0)), full(subkeys),
                  u_block(lambda e: 0),
                  u_block(lambda e: 2 * e + 1),
                  u_block(lambda e: jnp.minimum(2 * e + 2, nb - 1)),
                  pl.BlockSpec((None, 2, d, eb), lambda t, e: (layer, e, 0, 0))],
        out_specs=row,
        out_shape=jax.ShapeDtypeStruct((n, d), F32),
        scratch_shapes=[pltpu.VMEM((hp2, lt, PEER_NKEYS, LANES), F32),
                        pltpu.VMEM((PEER_HEADS, lt, PEER_NKEYS, LANES), F32),
                        pltpu.VMEM((PEER_HEADS, lt, PEER_NKEYS, LANES), F32),
                        pltpu.VMEM((PEER_HEADS, lt, PEER_NKEYS, LANES), GATE_DTYPE),
                        pltpu.VMEM((PEER_HEADS, lt, PEER_NKEYS, LANES), GATE_DTYPE),
                        pltpu.VMEM((PEER_HEADS, lt, 2 * SUBLANES, LANES), F32),
                        pltpu.VMEM((PEER_HEADS, lt, 2 * SUBLANES, LANES), F32),
                        pltpu.VMEM((d, tm), MXU_DTYPE),
                        pltpu.VMEM((hp2 * PEER_NKEYS, tm), MXU_DTYPE)]
                       + [pltpu.VMEM((1, PEER_PAIR, tm), F32)] * 8
                       + [pltpu.VMEM((1, eb, tm), MXU_DTYPE)] * 2
                       + [pltpu.VMEM((d, tm), F32)],
        compiler_params=_cparams(("parallel", "arbitrary"), PEER_SCHEDULER_FLAGS),
    )(jnp.zeros((1,), jnp.int32), h2, xm, mod, final_g, wq_t, subkeys, u_tab, u_tab, u_tab, vt_tab)


def kernel(x, c, ctx, c_ctx, ada_w, ada_b, norm1_g, w_in, conv_w, sgu_w, sgu_b, sgu_norm_g, na_rpb,
           out_norm_g, w_out, norm2_g, peer_wq, peer_subkeys, peer_u, peer_v, final_norm_g):
    batch, seq, d = x.shape
    depth = ada_w.shape[0]
    assert d == D_MODEL and batch == 2 and ctx.shape[1] == CTX_LEN
    assert seq % TOKEN_TILE == 0 and batch * CTX_LEN == TOKEN_TILE
    rows = seq // GRID_W
    assert rows % QUERY_ROWS == 0 and rows >= KEY_ROWS + QUERY_ROWS
    n_latent = batch * seq
    tiles_per_batch = seq // TOKEN_TILE

    xa = jnp.concatenate([x.reshape(n_latent, d), ctx.reshape(batch * CTX_LEN, d)], axis=0)
    cvec = jnp.zeros((SUBLANES, d), F32).at[0:batch].set(c).at[batch].set(c_ctx)
    mods = _adaln(cvec, ada_w, ada_b)
    bias = _build_bias(na_rpb, _bias_block_index(rows))

    mx = MXU_DTYPE
    w_in_c = w_in.astype(mx)
    w_out_c = w_out.astype(mx)
    sgu_w_c = sgu_w.astype(mx)
    sgu_bias = jnp.repeat(jnp.swapaxes(sgu_b, 1, 2), GROUP_DIM, axis=2)
    wq_t = jnp.swapaxes(peer_wq, 1, 2).astype(mx)
    subkeys = peer_subkeys.reshape(depth, 2 * PEER_HEADS, PEER_NKEYS, PEER_DKEY // 2).astype(mx)
    n_blocks = peer_u.shape[1] // PEER_EXPERT_BLOCK
    u_c = peer_u.astype(mx).reshape(depth, n_blocks, PEER_EXPERT_BLOCK, d)
    vt_c = jnp.swapaxes(peer_v.astype(mx).reshape(depth, n_blocks, PEER_EXPERT_BLOCK, d), 2, 3)

    for l in range(depth):
        mod = mods[l]
        gb, z, ysgu, qkv = _proj(xa, mod, norm1_g[l][None], w_in_c[l], sgu_w_c[l], sgu_bias[l],
                                 sgu_norm_g[l][None], tiles_per_batch)
        yna = _attention(qkv, bias, l, batch, seq)
        xm, h2 = _merge(gb, z, ysgu, yna, xa, mod, conv_w[l], out_norm_g[l][None], w_out_c[l],
                        norm2_g[l][None], tiles_per_batch, seq, n_latent)
        last = l == depth - 1
        xa = _peer(h2, xm, mod, final_norm_g[None], wq_t, subkeys[l], u_c, vt_c, l, tiles_per_batch,
                   n_latent if last else xm.shape[0], last)
    return xa.reshape(batch, seq, d)
```

```python
import functools

import numpy as np
import jax
import jax.numpy as jnp
from jax import lax
from jax.experimental import pallas as pl
from jax.experimental.pallas import tpu as pltpu

D_MODEL = 1024
GRID_W = 64
CTX_LEN = 256
EPS = 1e-6
GROUP_DIM = 64
CONV_W = 256
SGU_W = 256
SGU_GROUPS = 4
CHUNK = 128
NA_W = 512
NA_KH = 8
NA_KW = 16
MIX_COLS = 3 * CONV_W + 2 * SGU_W
PEER_HEADS = 8
PEER_NKEYS = 128
PEER_TOPK = 16
PEER_DKEY = 256

LANES = 128
SUBLANES = 8
TOKEN_TILE = 512
QUERY_ROWS = 4
QUERY_BLOCK = QUERY_ROWS * GRID_W
KEY_ROWS = 12
KEY_BLOCK = KEY_ROWS * GRID_W
MASKED = -1e30
PEER_EXPERT_BLOCK = 1024
PEER_PAIR = 2 * PEER_NKEYS
PEER_KEYS_PER_PASS = 2
PEER_OUT_CHUNKS = 4
VMEM_LIMIT = 60 * 1024 * 1024

MXU_DTYPE = jnp.bfloat16
GATE_DTYPE = jnp.bfloat16
F32 = jnp.float32
NT_DIMS = (((1,), (1,)), ((), ()))


def _cparams(sem):
    return pltpu.CompilerParams(dimension_semantics=sem, vmem_limit_bytes=VMEM_LIMIT)


def _rms(x, g):
    ms = jnp.mean(x * x, axis=-1, keepdims=True)
    return x * lax.rsqrt(ms + EPS) * g


def _group_rms128(y, g, lo):
    y2 = y * y
    s_lo = jnp.sum(jnp.where(lo, y2, 0.0), axis=-1, keepdims=True)
    s_hi = jnp.sum(jnp.where(lo, 0.0, y2), axis=-1, keepdims=True)
    inv = 1.0 / GROUP_DIM
    r = jnp.where(lo, lax.rsqrt(s_lo * inv + EPS), lax.rsqrt(s_hi * inv + EPS))
    return y * r * g


def _lo_mask():
    return lax.broadcasted_iota(jnp.int32, (1, LANES), 1) < GROUP_DIM


def _mod_row(mod_ref, tiles_per_batch):
    g = jnp.minimum(pl.program_id(0) // tiles_per_batch, 2)
    return mod_ref[pl.ds(g, 1), :]


def _adaln_kernel(c_ref, w_ref, b_ref, o_ref):
    cv = c_ref[...]
    s = cv * (1.0 / (1.0 + jnp.exp(-cv)))
    o_ref[...] = jnp.dot(s.astype(MXU_DTYPE), w_ref[...].astype(MXU_DTYPE),
                         preferred_element_type=F32) + b_ref[...]


def _adaln(cvec, ada_w, ada_b):
    depth, d, cols = ada_w.shape
    cb = 1536
    return pl.pallas_call(
        _adaln_kernel,
        grid=(depth, cols // cb),
        in_specs=[pl.BlockSpec((SUBLANES, d), lambda l, j: (0, 0)),
                  pl.BlockSpec((None, d, cb), lambda l, j: (l, 0, j)),
                  pl.BlockSpec((None, 1, cb), lambda l, j: (l, 0, j))],
        out_specs=pl.BlockSpec((None, SUBLANES, cb), lambda l, j: (l, 0, j)),
        out_shape=jax.ShapeDtypeStruct((depth, SUBLANES, cols), F32),
        compiler_params=_cparams(("parallel", "parallel")),
    )(cvec, ada_w, ada_b.reshape(depth, 1, cols))


def _proj_kernel(x_ref, mod_ref, n1g_ref, w_ref, sw_ref, sb_ref, sg_ref,
                 gb_ref, z_ref, ysgu_ref, qkv_ref, *, tiles_per_batch):
    d = D_MODEL
    mrow = _mod_row(mod_ref, tiles_per_batch)
    sh1, sc1 = mrow[:, 0:d], mrow[:, d:2 * d]
    h = _rms(x_ref[...], n1g_ref[...]) * (1.0 + sc1) + sh1
    p = jnp.dot(h.astype(MXU_DTYPE), w_ref[...], preferred_element_type=F32)
    gb_ref[...] = p[:, 0:CONV_W]
    z_ref[...] = p[:, CONV_W:2 * CONV_W] * p[:, 2 * CONV_W:3 * CONV_W]
    sg = jax.nn.gelu(p[:, 3 * CONV_W:MIX_COLS])
    u, v = sg[:, :SGU_W], sg[:, SGU_W:]
    lo = _lo_mask()
    vn = jnp.concatenate(
        [_group_rms128(v[:, :LANES], sg_ref[:, :LANES], lo),
         _group_rms128(v[:, LANES:], sg_ref[:, LANES:], lo)], axis=1).astype(MXU_DTYPE)
    grp = lax.broadcasted_iota(jnp.int32, (1, SGU_W), 1) // GROUP_DIM
    for c in range(TOKEN_TILE // CHUNK):
        rows = slice(c * CHUNK, (c + 1) * CHUNK)
        vc = vn[rows, :]
        mixed = jnp.zeros((CHUNK, SGU_W), F32)
        for gi in range(SGU_GROUPS):
            mg = jnp.dot(sw_ref[gi], vc, preferred_element_type=F32)
            mixed = jnp.where(grp == gi, mg, mixed)
        ysgu_ref[rows, :] = u[rows, :] * (mixed + sb_ref[...])
    qkv_ref[...] = p[:, MIX_COLS:].astype(qkv_ref.dtype)


def _proj(xa, mod, n1g, w_in, sgu_w, sgu_bias, sgu_g, tiles_per_batch):
    n, d = xa.shape
    tm = TOKEN_TILE
    row = lambda w: pl.BlockSpec((tm, w), lambda t: (t, 0))
    full = lambda a: pl.BlockSpec(a.shape, lambda t: (0,) * a.ndim)
    return pl.pallas_call(
        functools.partial(_proj_kernel, tiles_per_batch=tiles_per_batch),
        grid=(n // tm,),
        in_specs=[row(d), full(mod), full(n1g), full(w_in), full(sgu_w), full(sgu_bias), full(sgu_g)],
        out_specs=[row(CONV_W), row(CONV_W), row(SGU_W), row(3 * NA_W)],
        out_shape=[jax.ShapeDtypeStruct((n, CONV_W), F32), jax.ShapeDtypeStruct((n, CONV_W), F32),
                   jax.ShapeDtypeStruct((n, SGU_W), F32), jax.ShapeDtypeStruct((n, 3 * NA_W), MXU_DTYPE)],
        compiler_params=_cparams(("parallel",)),
    )(xa, mod, n1g, w_in, sgu_w, sgu_bias, sgu_g)


def _attn_kernel(q_ref, k_ref, v_ref, kc_ref, vc_ref, bias_ref, o_ref, *, rows):
    j = pl.program_id(2)
    kr0 = jnp.clip(QUERY_ROWS * j - NA_KH // 2, 0, rows - KEY_ROWS)
    start = pl.multiple_of(kr0 * GRID_W, GRID_W)
    kl = k_ref[pl.ds(start, KEY_BLOCK), :]
    vl = v_ref[pl.ds(start, KEY_BLOCK), :]
    kc, vc, q = kc_ref[...], vc_ref[...], q_ref[...]
    lo = _lo_mask()
    scale = GROUP_DIM ** -0.5
    o = jnp.zeros((QUERY_BLOCK, LANES), F32)
    for hh in range(2):
        hm = lo if hh == 0 else jnp.logical_not(lo)
        qh = jnp.where(hm, q, jnp.zeros_like(q)) * scale
        s_loc = lax.dot_general(qh, kl, NT_DIMS, preferred_element_type=F32) + bias_ref[0, hh]
        s_ctx = lax.dot_general(qh, kc, NT_DIMS, preferred_element_type=F32)
        m = jnp.maximum(jnp.max(s_loc, axis=-1, keepdims=True), jnp.max(s_ctx, axis=-1, keepdims=True))
        p_loc = jnp.exp(s_loc - m)
        p_ctx = jnp.exp(s_ctx - m)
        den = jnp.sum(p_loc, axis=-1, keepdims=True) + jnp.sum(p_ctx, axis=-1, keepdims=True)
        vlh = jnp.where(hm, vl, jnp.zeros_like(vl))
        vch = jnp.where(hm, vc, jnp.zeros_like(vc))
        oh = (jnp.dot(p_loc.astype(MXU_DTYPE), vlh, preferred_element_type=F32)
              + jnp.dot(p_ctx.astype(MXU_DTYPE), vch, preferred_element_type=F32))
        o = o + oh * (1.0 / den)
    o_ref[...] = o


def _attention(qkv, bias, layer, batch, seq):
    n = qkv.shape[0]
    rows = seq // GRID_W
    nblk = rows // QUERY_ROWS
    ctx_q0 = batch * seq // QUERY_BLOCK
    heads_cols = NA_W // LANES

    def q_map(b, hp, j):
        return (jnp.where(j < nblk, b * nblk + j, ctx_q0 + b), hp)

    def variant(b, hp, j):
        v = jnp.where(j == 0, 0, jnp.where(j < nblk - 1, 1, jnp.where(j == nblk - 1, 2, 3)))
        return (layer, v, hp, 0, 0)

    return pl.pallas_call(
        functools.partial(_attn_kernel, rows=rows),
        grid=(batch, heads_cols, nblk + 1),
        in_specs=[pl.BlockSpec((QUERY_BLOCK, LANES), q_map),
                  pl.BlockSpec((seq, LANES), lambda b, hp, j: (b, heads_cols + hp)),
                  pl.BlockSpec((seq, LANES), lambda b, hp, j: (b, 2 * heads_cols + hp)),
                  pl.BlockSpec((CTX_LEN, LANES), lambda b, hp, j: (ctx_q0 + b, heads_cols + hp)),
                  pl.BlockSpec((CTX_LEN, LANES), lambda b, hp, j: (ctx_q0 + b, 2 * heads_cols + hp)),
                  pl.BlockSpec((None, 1, 2, QUERY_BLOCK, KEY_BLOCK), variant)],
        out_specs=pl.BlockSpec((QUERY_BLOCK, LANES), q_map),
        out_shape=jax.ShapeDtypeStruct((n, NA_W), F32),
        compiler_params=_cparams(("parallel", "parallel", "arbitrary")),
    )(qkv, qkv, qkv, qkv, qkv, bias)


def _bias_block_index(rows):
    kh = min(NA_KH, rows)
    idx = np.full((4, QUERY_ROWS, KEY_ROWS), 2 * NA_KH - 1, np.int32)
    for v, (r0, kr0) in enumerate(((0, 0), (QUERY_ROWS, 0), (rows - QUERY_ROWS, rows - KEY_ROWS))):
        for qr in range(QUERY_ROWS):
            r = r0 + qr
            rs = min(max(r - kh // 2, 0), rows - kh)
            for kr in range(KEY_ROWS):
                if rs <= kr0 + kr < rs + kh:
                    idx[v, qr, kr] = kr0 + kr - r + NA_KH - 1
    return idx


def _build_bias(rpb, block_index):
    depth, heads = rpb.shape[:2]
    pad = GRID_W - NA_KW
    padded = jnp.pad(rpb, ((0, 0), (0, 0), (0, 0), (pad, pad)), constant_values=MASKED)
    toe = jnp.stack([padded[..., GRID_W - 1 - qc:2 * GRID_W - 1 - qc] for qc in range(GRID_W)], axis=3)
    qc = np.arange(GRID_W)[:, None]
    kc = np.arange(GRID_W)[None, :]
    cs = np.clip(qc - NA_KW // 2, 0, GRID_W - NA_KW)
    toe = jnp.where((kc >= cs) & (kc < cs + NA_KW), toe, MASKED)
    toe = jnp.concatenate([toe, jnp.full((depth, heads, 1, GRID_W, GRID_W), MASKED, toe.dtype)], axis=2)
    flat = [int(a) for a in block_index.reshape(-1)]
    blocks = jnp.stack([toe[:, :, a] for a in flat], axis=2)
    blocks = blocks.reshape(depth, heads, 4, QUERY_ROWS, KEY_ROWS, GRID_W, GRID_W)
    return jnp.transpose(blocks, (0, 2, 1, 3, 5, 4, 6)).reshape(depth, 4, heads, QUERY_BLOCK, KEY_BLOCK)


def _merge_kernel(gb_ref, z_ref, zp_ref, zn_ref, ysgu_ref, yna_ref, x_ref, mod_ref, cw_ref, og_ref,
                  w_ref, n2g_ref, xo_ref, h2_ref, *, tiles_per_batch, seq, n_latent):
    d = D_MODEL
    tm = TOKEN_TILE
    t = pl.program_id(0)
    mrow = _mod_row(mod_ref, tiles_per_batch)
    g1, sh2, sc2 = mrow[:, 2 * d:3 * d], mrow[:, 3 * d:4 * d], mrow[:, 4 * d:5 * d]
    z = z_ref[...]
    ridx = lax.broadcasted_iota(jnp.int32, (tm, 1), 0)
    grow = t * tm + ridx
    seqlen = jnp.where(t * tm >= n_latent, CTX_LEN, seq)
    pos = grow & (seqlen - 1)
    z_prev = jnp.where(ridx == 0, zp_ref[SUBLANES - 1:SUBLANES, :], pltpu.roll(z, 1, 0))
    z_next = jnp.where(ridx == tm - 1, zn_ref[0:1, :], pltpu.roll(z, tm - 1, 0))
    z_prev = jnp.where(pos == 0, 0.0, z_prev)
    z_next = jnp.where(pos == seqlen - 1, 0.0, z_next)
    conv = z_prev * cw_ref[0:1, :] + z * cw_ref[1:2, :] + z_next * cw_ref[2:3, :]
    y_conv = gb_ref[...] * conv
    lo = _lo_mask()
    blocks = []
    for src, width in ((y_conv, CONV_W), (ysgu_ref[...], SGU_W), (yna_ref[...], NA_W)):
        for c in range(width // LANES):
            off = len(blocks) * LANES
            blocks.append(_group_rms128(src[:, c * LANES:(c + 1) * LANES], og_ref[:, off:off + LANES], lo)
                          .astype(MXU_DTYPE))
    y = jnp.dot(jnp.concatenate(blocks, axis=1), w_ref[...], preferred_element_type=F32)
    xm = x_ref[...] + g1 * y
    xo_ref[...] = xm
    h2_ref[...] = (_rms(xm, n2g_ref[...]) * (1.0 + sc2) + sh2).astype(h2_ref.dtype)


def _merge(gb, z, ysgu, yna, xa, mod, conv_w, out_g, w_out, n2g, tiles_per_batch, seq, n_latent):
    n, d = xa.shape
    tm = TOKEN_TILE
    hb = tm // SUBLANES
    nb8 = n // SUBLANES
    row = lambda w: pl.BlockSpec((tm, w), lambda t: (t, 0))
    full = lambda a: pl.BlockSpec(a.shape, lambda t: (0,) * a.ndim)
    halo_prev = pl.BlockSpec((SUBLANES, CONV_W), lambda t: (jnp.maximum(t * hb - 1, 0), 0))
    halo_next = pl.BlockSpec((SUBLANES, CONV_W), lambda t: (jnp.minimum((t + 1) * hb, nb8 - 1), 0))
    return pl.pallas_call(
        functools.partial(_merge_kernel, tiles_per_batch=tiles_per_batch, seq=seq, n_latent=n_latent),
        grid=(n // tm,),
        in_specs=[row(CONV_W), row(CONV_W), halo_prev, halo_next, row(SGU_W), row(NA_W), row(d),
                  full(mod), full(conv_w), full(out_g), full(w_out), full(n2g)],
        out_specs=[row(d), row(d)],
        out_shape=[jax.ShapeDtypeStruct((n, d), F32), jax.ShapeDtypeStruct((n, d), MXU_DTYPE)],
        compiler_params=_cparams(("parallel",)),
    )(gb, z, z, z, ysgu, yna, xa, mod, conv_w, out_g, w_out, n2g)


def _odd_even_merge(lo, hi, r):
    step = r * 2
    if step < hi - lo:
        yield from _odd_even_merge(lo, hi, step)
        yield from _odd_even_merge(lo + r, hi, step)
        for i in range(lo + r, hi - r, step):
            yield (i, i + r)
    else:
        yield (lo, lo + r)


def _odd_even_sort(lo, hi):
    if hi - lo >= 1:
        mid = lo + (hi - lo) // 2
        yield from _odd_even_sort(lo, mid)
        yield from _odd_even_sort(mid + 1, hi)
        yield from _odd_even_merge(lo, hi, 1)


def _sort_desc(vals):
    for i, j in _odd_even_sort(0, len(vals) - 1):
        vals[i], vals[j] = jnp.maximum(vals[i], vals[j]), jnp.minimum(vals[i], vals[j])
    return vals


def _merge_bitonic_desc(vals):
    n = len(vals)
    k = n // 2
    while k >= 1:
        for i in range(n):
            if (i & k) == 0:
                hi, lo = jnp.maximum(vals[i], vals[i + k]), jnp.minimum(vals[i], vals[i + k])
                vals[i], vals[i + k] = hi, lo
        k //= 2
    return vals


def _top_merge(a, b):
    n = len(a)
    return _merge_bitonic_desc([jnp.maximum(a[i], b[n - 1 - i]) for i in range(n)])


def _top16_over_keys(tiles):
    vals = _sort_desc(list(tiles))
    for shift in (4, 2, 1):
        vals = _top_merge(vals, [pltpu.roll(v, shift, 0) for v in vals])
    return vals


def _pair_threshold(a, b):
    k = PEER_TOPK
    rows = [[a[i] + b[j] for j in range(k // (i + 1))] for i in range(k)]
    neg = jnp.full(a[0].shape, -jnp.inf, F32)

    def merge(x, y, n):
        return _merge_bitonic_desc(x + [neg] * (n - len(x) - len(y)) + y[::-1])

    l1 = merge(rows[1], rows[2], k)
    l2 = merge(merge(rows[3], rows[4], 8), merge(merge(rows[5], rows[6], 4), rows[7], 8), k)
    l3 = [r[0] for r in rows[8:]] + [neg] * (k // 2)
    return _top_merge(_top_merge(rows[0], l1), _top_merge(l2, l3))


def _sorted_count(b, pred):
    one = lambda p, v: jnp.where(p, v, 0.0)
    p8 = pred(b[7])
    p4 = pred(jnp.where(p8, b[11], b[3]))
    p2 = pred(jnp.where(p8, jnp.where(p4, b[13], b[9]), jnp.where(p4, b[5], b[1])))
    lo = jnp.where(p4, jnp.where(p2, b[6], b[4]), jnp.where(p2, b[2], b[0]))
    hi = jnp.where(p4, jnp.where(p2, b[14], b[12]), jnp.where(p2, b[10], b[8]))
    p1 = pred(jnp.where(p8, hi, lo))
    p0 = pred(b[15])
    return one(p8, 8.0) + one(p4, 4.0) + one(p2, 2.0) + one(p1, 1.0) + one(p0, 1.0)


def _gelu_tanh(x):
    k1 = -2.0 * (2.0 / np.pi) ** 0.5
    k2 = k1 * 0.044715
    return x / (1.0 + jnp.exp(x * (k1 + k2 * (x * x))))


def _peer_kernel(zero_ref, h2_ref, x_ref, mod_ref, fg_ref, wq_ref, sk_ref, u_first_ref, u_odd_ref, u_next_ref,
                 vt_ref, o_ref, s_sc, cnt_sc, e0_sc, rk_sc, e1_sc, cnt_st, e0_st, h2t_sc, qt_sc,
                 act_e0, act_e1, act_e2, act_e3, act_o0, act_o1, act_o2, act_o3, w_e, w_o, acc_sc,
                 *, tiles_per_batch, final_norm):
    d = D_MODEL
    tm = TOKEN_TILE
    nk = PEER_NKEYS
    key_tiles = nk // SUBLANES
    lane_tiles = tm // LANES
    packed = 2 * SUBLANES
    gate_tiles = nk // packed
    n_pairs = PEER_EXPERT_BLOCK // PEER_PAIR
    e = pl.program_id(1)
    gdt = rk_sc.dtype
    assert PEER_EXPERT_BLOCK == SUBLANES * nk

    @pl.when(e == 0)
    def _scores_and_thresholds():
        h2t_sc[...] = h2_ref[...].astype(F32).T.astype(h2t_sc.dtype)
        for ip, buf in enumerate((act_e0, act_e1, act_e2, act_e3)):
            buf[0] = jnp.dot(u_first_ref[ip * PEER_PAIR:(ip + 1) * PEER_PAIR, :], h2t_sc[...],
                             preferred_element_type=F32)
        q_t = jnp.dot(wq_ref[...], h2t_sc[...], preferred_element_type=F32)
        qt_sc[...] = q_t.astype(qt_sc.dtype)
        for hp in range(2 * PEER_HEADS):
            s_t = jnp.dot(sk_ref[hp], qt_sc[hp * nk:(hp + 1) * nk, :], preferred_element_type=F32)
            for lt in range(lane_tiles):
                s_sc[hp, lt] = s_t[:, lt * LANES:(lt + 1) * LANES]
        acc_sc[...] = jnp.zeros_like(acc_sc)

        def head_tile(it, carry):
            h = it // lane_tiles
            lt = it % lane_tiles
            rows = lambda v: slice(v * SUBLANES, (v + 1) * SUBLANES)
            s0 = [s_sc[2 * h, lt, rows(v), :] for v in range(key_tiles)]
            s1 = [s_sc[2 * h + 1, lt, rows(v), :] for v in range(key_tiles)]
            a = _top16_over_keys(s0)
            b = _top16_over_keys(s1)
            best = _pair_threshold(a, b)
            tau = best[PEER_TOPK - 1]
            z = jnp.ones_like(tau)
            for kk in range(1, PEER_TOPK):
                z = z + jnp.exp(best[kk] - best[0])
            rz = 1.0 / z
            for v in range(key_tiles):
                cnt_sc[h, lt, rows(v), :] = _sorted_count(b, lambda bk: s0[v] + bk >= tau)
                e0_sc[h, lt, rows(v), :] = jnp.exp(s0[v] - a[0])
            for vp in range(gate_tiles):
                rk, e1 = [], []
                for v in (2 * vp, 2 * vp + 1):
                    rk.append(_sorted_count(b, lambda bk: bk > s1[v]))
                    e1.append(jnp.exp(s1[v] - b[0]) * rz)
                prow = slice(vp * packed, (vp + 1) * packed)
                rk_sc[h, lt, prow, :] = jnp.concatenate(rk, axis=0).astype(gdt)
                e1_sc[h, lt, prow, :] = jnp.concatenate(e1, axis=0).astype(gdt)
            return carry

        lax.fori_loop(0, PEER_HEADS * lane_tiles, head_tile, 0)

    base = pl.multiple_of(e * 2 * SUBLANES, 2 * SUBLANES)
    for h in range(PEER_HEADS):
        for lt in range(lane_tiles):
            cnt_st[h, lt] = cnt_sc[h, lt, pl.ds(base, 2 * SUBLANES), :]
            e0_st[h, lt] = e0_sc[h, lt, pl.ds(base, 2 * SUBLANES), :]
    z0 = zero_ref[0]

    def pre_activation(u_block, bufs, ip):
        bufs[ip][0] = jnp.dot(u_block[ip * PEER_PAIR:(ip + 1) * PEER_PAIR, :], h2t_sc[...],
                              preferred_element_type=F32)

    def gated_activation(s, bufs, w_buf, ip):
        kpp = PEER_KEYS_PER_PASS
        for lt in range(lane_tiles):
            lanes = slice(lt * LANES, (lt + 1) * LANES)
            for i0 in range(0, 2, kpp):
                gate = [[jnp.zeros((packed, LANES), gdt) for _ in range(gate_tiles)] for _ in range(kpp)]
                for h in range(PEER_HEADS):
                    def row(ref, ii):
                        r = s * SUBLANES + 2 * ip + i0 + ii
                        return jnp.broadcast_to(ref[h, lt, r:r + 1, :], (packed, LANES)).astype(gdt)
                    cnt = [row(cnt_st, ii) for ii in range(kpp)]
                    e0 = [row(e0_st, ii) for ii in range(kpp)]
                    for jp in range(gate_tiles):
                        prow = slice(jp * packed, (jp + 1) * packed)
                        rk = rk_sc[h, lt, prow, :]
                        e1 = e1_sc[h, lt, prow, :]
                        for ii in range(kpp):
                            gate[ii][jp] = gate[ii][jp] + jnp.where(rk < cnt[ii], e1 * e0[ii],
                                                                     jnp.zeros_like(e1))
                for ii in range(kpp):
                    for jp in range(gate_tiles):
                        r0 = (i0 + ii) * nk + jp * packed
                        act = _gelu_tanh(bufs[ip][z0, r0:r0 + packed, lanes].astype(gdt))
                        w0 = ip * PEER_PAIR + r0
                        w_buf[0, w0:w0 + packed, lanes] = (gate[ii][jp] * act).astype(w_buf.dtype)

    def output(s, w_buf):
        for ic in range(PEER_OUT_CHUNKS):
            rows = slice(ic * (d // PEER_OUT_CHUNKS), (ic + 1) * (d // PEER_OUT_CHUNKS))
            acc_sc[rows, :] += jnp.dot(vt_ref[s, rows, :], w_buf[z0], preferred_element_type=F32)

    even = (act_e0, act_e1, act_e2, act_e3)
    odd = (act_o0, act_o1, act_o2, act_o3)
    for ip in range(n_pairs):
        pre_activation(u_odd_ref, odd, ip)
        gated_activation(0, even, w_e, ip)
    output(0, w_e)
    for ip in range(n_pairs):
        pre_activation(u_next_ref, even, ip)
        gated_activation(1, odd, w_o, ip)
    output(1, w_o)

    @pl.when(e == pl.num_programs(1) - 1)
    def _residual():
        g2 = _mod_row(mod_ref, tiles_per_batch)[:, 5 * d:6 * d]
        y = x_ref[...] + g2 * acc_sc[...].T
        o_ref[...] = _rms(y, fg_ref[...]) if final_norm else y


def _peer(h2, xm, mod, final_g, wq_t, subkeys, u_tab, vt_tab, layer, tiles_per_batch, n, final_norm):
    d = xm.shape[1]
    tm = TOKEN_TILE
    eb = PEER_EXPERT_BLOCK
    nb = u_tab.shape[1]
    full = lambda a: pl.BlockSpec(a.shape, lambda t, e: (0,) * a.ndim)
    row = pl.BlockSpec((tm, d), lambda t, e: (t, 0))
    u_block = lambda blk: pl.BlockSpec((None, None, eb, d), lambda t, e: (layer, blk(e), 0, 0))
    hp2 = 2 * PEER_HEADS
    lt = tm // LANES
    return pl.pallas_call(
        functools.partial(_peer_kernel, tiles_per_batch=tiles_per_batch, final_norm=final_norm),
        grid=(n // tm, nb // 2),
        in_specs=[pl.BlockSpec(memory_space=pltpu.SMEM), row, row, full(mod), full(final_g),
                  pl.BlockSpec((None,) + wq_t.shape[1:], lambda t, e: (layer, 0, 0)), full(subkeys),
                  u_block(lambda e: 0),
                  u_block(lambda e: 2 * e + 1),
                  u_block(lambda e: jnp.minimum(2 * e + 2, nb - 1)),
                  pl.BlockSpec((None, 2, d, eb), lambda t, e: (layer, e, 0, 0))],
        out_specs=row,
        out_shape=jax.ShapeDtypeStruct((n, d), F32),
        scratch_shapes=[pltpu.VMEM((hp2, lt, PEER_NKEYS, LANES), F32),
                        pltpu.VMEM((PEER_HEADS, lt, PEER_NKEYS, LANES), F32),
                        pltpu.VMEM((PEER_HEADS, lt, PEER_NKEYS, LANES), F32),
                        pltpu.VMEM((PEER_HEADS, lt, PEER_NKEYS, LANES), GATE_DTYPE),
                        pltpu.VMEM((PEER_HEADS, lt, PEER_NKEYS, LANES), GATE_DTYPE),
                        pltpu.VMEM((PEER_HEADS, lt, 2 * SUBLANES, LANES), F32),
                        pltpu.VMEM((PEER_HEADS, lt, 2 * SUBLANES, LANES), F32),
                        pltpu.VMEM((d, tm), MXU_DTYPE),
                        pltpu.VMEM((hp2 * PEER_NKEYS, tm), MXU_DTYPE)]
                       + [pltpu.VMEM((1, PEER_PAIR, tm), F32)] * 8
                       + [pltpu.VMEM((1, eb, tm), MXU_DTYPE)] * 2
                       + [pltpu.VMEM((d, tm), F32)],
        compiler_params=_cparams(("parallel", "arbitrary")),
    )(jnp.zeros((1,), jnp.int32), h2, xm, mod, final_g, wq_t, subkeys, u_tab, u_tab, u_tab, vt_tab)


def kernel(x, c, ctx, c_ctx, ada_w, ada_b, norm1_g, w_in, conv_w, sgu_w, sgu_b, sgu_norm_g, na_rpb,
           out_norm_g, w_out, norm2_g, peer_wq, peer_subkeys, peer_u, peer_v, final_norm_g):
    batch, seq, d = x.shape
    depth = ada_w.shape[0]
    assert d == D_MODEL and batch == 2 and ctx.shape[1] == CTX_LEN
    assert seq % TOKEN_TILE == 0 and batch * CTX_LEN == TOKEN_TILE
    rows = seq // GRID_W
    assert rows % QUERY_ROWS == 0 and rows >= KEY_ROWS + QUERY_ROWS
    n_latent = batch * seq
    tiles_per_batch = seq // TOKEN_TILE

    xa = jnp.concatenate([x.reshape(n_latent, d), ctx.reshape(batch * CTX_LEN, d)], axis=0)
    cvec = jnp.zeros((SUBLANES, d), F32).at[0:batch].set(c).at[batch].set(c_ctx)
    mods = _adaln(cvec, ada_w, ada_b)
    bias = _build_bias(na_rpb, _bias_block_index(rows))

    mx = MXU_DTYPE
    w_in_c = w_in.astype(mx)
    w_out_c = w_out.astype(mx)
    sgu_w_c = sgu_w.astype(mx)
    sgu_bias = jnp.repeat(jnp.swapaxes(sgu_b, 1, 2), GROUP_DIM, axis=2)
    wq_t = jnp.swapaxes(peer_wq, 1, 2).astype(mx)
    subkeys = peer_subkeys.reshape(depth, 2 * PEER_HEADS, PEER_NKEYS, PEER_DKEY // 2).astype(mx)
    n_blocks = peer_u.shape[1] // PEER_EXPERT_BLOCK
    u_c = peer_u.astype(mx).reshape(depth, n_blocks, PEER_EXPERT_BLOCK, d)
    vt_c = jnp.swapaxes(peer_v.astype(mx).reshape(depth, n_blocks, PEER_EXPERT_BLOCK, d), 2, 3)

    for l in range(depth):
        mod = mods[l]
        gb, z, ysgu, qkv = _proj(xa, mod, norm1_g[l][None], w_in_c[l], sgu_w_c[l], sgu_bias[l],
                                 sgu_norm_g[l][None], tiles_per_batch)
        yna = _attention(qkv, bias, l, batch, seq)
        xm, h2 = _merge(gb, z, ysgu, yna, xa, mod, conv_w[l], out_norm_g[l][None], w_out_c[l],
                        norm2_g[l][None], tiles_per_batch, seq, n_latent)
        last = l == depth - 1
        xa = _peer(h2, xm, mod, final_norm_g[None], wq_t, subkeys[l], u_c, vt_c, l, tiles_per_batch,
                   n_latent if last else xm.shape[0], last)
    return xa.reshape(batch, seq, d)
```

```python
import functools

import numpy as np
import jax
import jax.numpy as jnp
from jax import lax
from jax.experimental import pallas as pl
from jax.experimental.pallas import tpu as pltpu

D_MODEL = 1024
GRID_W = 64
CTX_LEN = 256
EPS = 1e-6
GROUP_DIM = 64
CONV_W = 256
SGU_W = 256
SGU_GROUPS = 4
CHUNK = 128
NA_W = 512
NA_KH = 8
NA_KW = 16
MIX_COLS = 3 * CONV_W + 2 * SGU_W
PEER_HEADS = 8
PEER_NKEYS = 128
PEER_TOPK = 16
PEER_DKEY = 256

LANES = 128
SUBLANES = 8
TOKEN_TILE = 512
QUERY_ROWS = 4
QUERY_BLOCK = QUERY_ROWS * GRID_W
KEY_ROWS = 12
KEY_BLOCK = KEY_ROWS * GRID_W
MASKED = -1e30
PEER_EXPERT_BLOCK = 1024
PEER_PAIR = 2 * PEER_NKEYS
PEER_KEYS_PER_PASS = 2
PEER_OUT_CHUNKS = 4
VMEM_LIMIT = 60 * 1024 * 1024

MXU_DTYPE = jnp.bfloat16
F32 = jnp.float32
NT_DIMS = (((1,), (1,)), ((), ()))


def _cparams(sem):
    return pltpu.CompilerParams(dimension_semantics=sem, vmem_limit_bytes=VMEM_LIMIT)


def _rms(x, g):
    ms = jnp.mean(x * x, axis=-1, keepdims=True)
    return x * lax.rsqrt(ms + EPS) * g


def _group_rms128(y, g, lo):
    y2 = y * y
    s_lo = jnp.sum(jnp.where(lo, y2, 0.0), axis=-1, keepdims=True)
    s_hi = jnp.sum(jnp.where(lo, 0.0, y2), axis=-1, keepdims=True)
    inv = 1.0 / GROUP_DIM
    r = jnp.where(lo, lax.rsqrt(s_lo * inv + EPS), lax.rsqrt(s_hi * inv + EPS))
    return y * r * g


def _lo_mask():
    return lax.broadcasted_iota(jnp.int32, (1, LANES), 1) < GROUP_DIM


def _mod_row(mod_ref, tiles_per_batch):
    g = jnp.minimum(pl.program_id(0) // tiles_per_batch, 2)
    return mod_ref[pl.ds(g, 1), :]


def _adaln_kernel(c_ref, w_ref, b_ref, o_ref):
    cv = c_ref[...]
    s = cv * (1.0 / (1.0 + jnp.exp(-cv)))
    o_ref[...] = jnp.dot(s.astype(MXU_DTYPE), w_ref[...].astype(MXU_DTYPE),
                         preferred_element_type=F32) + b_ref[...]


def _adaln(cvec, ada_w, ada_b):
    depth, d, cols = ada_w.shape
    cb = 1536
    return pl.pallas_call(
        _adaln_kernel,
        grid=(depth, cols // cb),
        in_specs=[pl.BlockSpec((SUBLANES, d), lambda l, j: (0, 0)),
                  pl.BlockSpec((None, d, cb), lambda l, j: (l, 0, j)),
                  pl.BlockSpec((None, 1, cb), lambda l, j: (l, 0, j))],
        out_specs=pl.BlockSpec((None, SUBLANES, cb), lambda l, j: (l, 0, j)),
        out_shape=jax.ShapeDtypeStruct((depth, SUBLANES, cols), F32),
        compiler_params=_cparams(("parallel", "parallel")),
    )(cvec, ada_w, ada_b.reshape(depth, 1, cols))


def _proj_kernel(x_ref, mod_ref, n1g_ref, w_ref, sw_ref, sb_ref, sg_ref,
                 gb_ref, z_ref, ysgu_ref, qkv_ref, *, tiles_per_batch):
    d = D_MODEL
    mrow = _mod_row(mod_ref, tiles_per_batch)
    sh1, sc1 = mrow[:, 0:d], mrow[:, d:2 * d]
    h = _rms(x_ref[...], n1g_ref[...]) * (1.0 + sc1) + sh1
    p = jnp.dot(h.astype(MXU_DTYPE), w_ref[...], preferred_element_type=F32)
    gb_ref[...] = p[:, 0:CONV_W]
    z_ref[...] = p[:, CONV_W:2 * CONV_W] * p[:, 2 * CONV_W:3 * CONV_W]
    sg = jax.nn.gelu(p[:, 3 * CONV_W:MIX_COLS])
    u, v = sg[:, :SGU_W], sg[:, SGU_W:]
    lo = _lo_mask()
    vn = jnp.concatenate(
        [_group_rms128(v[:, :LANES], sg_ref[:, :LANES], lo),
         _group_rms128(v[:, LANES:], sg_ref[:, LANES:], lo)], axis=1).astype(MXU_DTYPE)
    grp = lax.broadcasted_iota(jnp.int32, (1, SGU_W), 1) // GROUP_DIM
    for c in range(TOKEN_TILE // CHUNK):
        rows = slice(c * CHUNK, (c + 1) * CHUNK)
        vc = vn[rows, :]
        mixed = jnp.zeros((CHUNK, SGU_W), F32)
        for gi in range(SGU_GROUPS):
            mg = jnp.dot(sw_ref[gi], vc, preferred_element_type=F32)
            mixed = jnp.where(grp == gi, mg, mixed)
        ysgu_ref[rows, :] = u[rows, :] * (mixed + sb_ref[...])
    qkv_ref[...] = p[:, MIX_COLS:].astype(qkv_ref.dtype)


def _proj(xa, mod, n1g, w_in, sgu_w, sgu_bias, sgu_g, tiles_per_batch):
    n, d = xa.shape
    tm = TOKEN_TILE
    row = lambda w: pl.BlockSpec((tm, w), lambda t: (t, 0))
    full = lambda a: pl.BlockSpec(a.shape, lambda t: (0,) * a.ndim)
    return pl.pallas_call(
        functools.partial(_proj_kernel, tiles_per_batch=tiles_per_batch),
        grid=(n // tm,),
        in_specs=[row(d), full(mod), full(n1g), full(w_in), full(sgu_w), full(sgu_bias), full(sgu_g)],
        out_specs=[row(CONV_W), row(CONV_W), row(SGU_W), row(3 * NA_W)],
        out_shape=[jax.ShapeDtypeStruct((n, CONV_W), F32), jax.ShapeDtypeStruct((n, CONV_W), F32),
                   jax.ShapeDtypeStruct((n, SGU_W), F32), jax.ShapeDtypeStruct((n, 3 * NA_W), MXU_DTYPE)],
        compiler_params=_cparams(("parallel",)),
    )(xa, mod, n1g, w_in, sgu_w, sgu_bias, sgu_g)


def _attn_kernel(q_ref, k_ref, v_ref, kc_ref, vc_ref, bias_ref, o_ref, *, rows):
    j = pl.program_id(2)
    kr0 = jnp.clip(QUERY_ROWS * j - NA_KH // 2, 0, rows - KEY_ROWS)
    start = pl.multiple_of(kr0 * GRID_W, GRID_W)
    kl = k_ref[pl.ds(start, KEY_BLOCK), :]
    vl = v_ref[pl.ds(start, KEY_BLOCK), :]
    kc, vc, q = kc_ref[...], vc_ref[...], q_ref[...]
    lo = _lo_mask()
    scale = GROUP_DIM ** -0.5
    o = jnp.zeros((QUERY_BLOCK, LANES), F32)
    for hh in range(2):
        hm = lo if hh == 0 else jnp.logical_not(lo)
        qh = jnp.where(hm, q, jnp.zeros_like(q)) * scale
        s_loc = lax.dot_general(qh, kl, NT_DIMS, preferred_element_type=F32) + bias_ref[0, hh]
        s_ctx = lax.dot_general(qh, kc, NT_DIMS, preferred_element_type=F32)
        m = jnp.maximum(jnp.max(s_loc, axis=-1, keepdims=True), jnp.max(s_ctx, axis=-1, keepdims=True))
        p_loc = jnp.exp(s_loc - m)
        p_ctx = jnp.exp(s_ctx - m)
        den = jnp.sum(p_loc, axis=-1, keepdims=True) + jnp.sum(p_ctx, axis=-1, keepdims=True)
        vlh = jnp.where(hm, vl, jnp.zeros_like(vl))
        vch = jnp.where(hm, vc, jnp.zeros_like(vc))
        oh = (jnp.dot(p_loc.astype(MXU_DTYPE), vlh, preferred_element_type=F32)
              + jnp.dot(p_ctx.astype(MXU_DTYPE), vch, preferred_element_type=F32))
        o = o + oh * (1.0 / den)
    o_ref[...] = o


def _attention(qkv, bias, layer, batch, seq):
    n = qkv.shape[0]
    rows = seq // GRID_W
    nblk = rows // QUERY_ROWS
    ctx_q0 = batch * seq // QUERY_BLOCK
    heads_cols = NA_W // LANES

    def q_map(b, hp, j):
        return (jnp.where(j < nblk, b * nblk + j, ctx_q0 + b), hp)

    def variant(b, hp, j):
        v = jnp.where(j == 0, 0, jnp.where(j < nblk - 1, 1, jnp.where(j == nblk - 1, 2, 3)))
        return (layer, v, hp, 0, 0)

    return pl.pallas_call(
        functools.partial(_attn_kernel, rows=rows),
        grid=(batch, heads_cols, nblk + 1),
        in_specs=[pl.BlockSpec((QUERY_BLOCK, LANES), q_map),
                  pl.BlockSpec((seq, LANES), lambda b, hp, j: (b, heads_cols + hp)),
                  pl.BlockSpec((seq, LANES), lambda b, hp, j: (b, 2 * heads_cols + hp)),
                  pl.BlockSpec((CTX_LEN, LANES), lambda b, hp, j: (ctx_q0 + b, heads_cols + hp)),
                  pl.BlockSpec((CTX_LEN, LANES), lambda b, hp, j: (ctx_q0 + b, 2 * heads_cols + hp)),
                  pl.BlockSpec((None, 1, 2, QUERY_BLOCK, KEY_BLOCK), variant)],
        out_specs=pl.BlockSpec((QUERY_BLOCK, LANES), q_map),
        out_shape=jax.ShapeDtypeStruct((n, NA_W), F32),
        compiler_params=_cparams(("parallel", "parallel", "arbitrary")),
    )(qkv, qkv, qkv, qkv, qkv, bias)


def _bias_block_index(rows):
    kh = min(NA_KH, rows)
    idx = np.full((4, QUERY_ROWS, KEY_ROWS), 2 * NA_KH - 1, np.int32)
    for v, (r0, kr0) in enumerate(((0, 0), (QUERY_ROWS, 0), (rows - QUERY_ROWS, rows - KEY_ROWS))):
        for qr in range(QUERY_ROWS):
            r = r0 + qr
            rs = min(max(r - kh // 2, 0), rows - kh)
            for kr in range(KEY_ROWS):
                if rs <= kr0 + kr < rs + kh:
                    idx[v, qr, kr] = kr0 + kr - r + NA_KH - 1
    return idx


def _build_bias(rpb, block_index):
    depth, heads = rpb.shape[:2]
    pad = GRID_W - NA_KW
    padded = jnp.pad(rpb, ((0, 0), (0, 0), (0, 0), (pad, pad)), constant_values=MASKED)
    toe = jnp.stack([padded[..., GRID_W - 1 - qc:2 * GRID_W - 1 - qc] for qc in range(GRID_W)], axis=3)
    qc = np.arange(GRID_W)[:, None]
    kc = np.arange(GRID_W)[None, :]
    cs = np.clip(qc - NA_KW // 2, 0, GRID_W - NA_KW)
    toe = jnp.where((kc >= cs) & (kc < cs + NA_KW), toe, MASKED)
    toe = jnp.concatenate([toe, jnp.full((depth, heads, 1, GRID_W, GRID_W), MASKED, toe.dtype)], axis=2)
    flat = [int(a) for a in block_index.reshape(-1)]
    blocks = jnp.stack([toe[:, :, a] for a in flat], axis=2)
    blocks = blocks.reshape(depth, heads, 4, QUERY_ROWS, KEY_ROWS, GRID_W, GRID_W)
    return jnp.transpose(blocks, (0, 2, 1, 3, 5, 4, 6)).reshape(depth, 4, heads, QUERY_BLOCK, KEY_BLOCK)


def _merge_kernel(gb_ref, z_ref, zp_ref, zn_ref, ysgu_ref, yna_ref, x_ref, mod_ref, cw_ref, og_ref,
                  w_ref, n2g_ref, xo_ref, h2_ref, *, tiles_per_batch, seq, n_latent):
    d = D_MODEL
    tm = TOKEN_TILE
    t = pl.program_id(0)
    mrow = _mod_row(mod_ref, tiles_per_batch)
    g1, sh2, sc2 = mrow[:, 2 * d:3 * d], mrow[:, 3 * d:4 * d], mrow[:, 4 * d:5 * d]
    z = z_ref[...]
    ridx = lax.broadcasted_iota(jnp.int32, (tm, 1), 0)
    grow = t * tm + ridx
    seqlen = jnp.where(t * tm >= n_latent, CTX_LEN, seq)
    pos = grow & (seqlen - 1)
    z_prev = jnp.where(ridx == 0, zp_ref[SUBLANES - 1:SUBLANES, :], pltpu.roll(z, 1, 0))
    z_next = jnp.where(ridx == tm - 1, zn_ref[0:1, :], pltpu.roll(z, tm - 1, 0))
    z_prev = jnp.where(pos == 0, 0.0, z_prev)
    z_next = jnp.where(pos == seqlen - 1, 0.0, z_next)
    conv = z_prev * cw_ref[0:1, :] + z * cw_ref[1:2, :] + z_next * cw_ref[2:3, :]
    y_conv = gb_ref[...] * conv
    lo = _lo_mask()
    blocks = []
    for src, width in ((y_conv, CONV_W), (ysgu_ref[...], SGU_W), (yna_ref[...], NA_W)):
        for c in range(width // LANES):
            off = len(blocks) * LANES
            blocks.append(_group_rms128(src[:, c * LANES:(c + 1) * LANES], og_ref[:, off:off + LANES], lo)
                          .astype(MXU_DTYPE))
    y = jnp.dot(jnp.concatenate(blocks, axis=1), w_ref[...], preferred_element_type=F32)
    xm = x_ref[...] + g1 * y
    xo_ref[...] = xm
    h2_ref[...] = (_rms(xm, n2g_ref[...]) * (1.0 + sc2) + sh2).astype(h2_ref.dtype)


def _merge(gb, z, ysgu, yna, xa, mod, conv_w, out_g, w_out, n2g, tiles_per_batch, seq, n_latent):
    n, d = xa.shape
    tm = TOKEN_TILE
    hb = tm // SUBLANES
    nb8 = n // SUBLANES
    row = lambda w: pl.BlockSpec((tm, w), lambda t: (t, 0))
    full = lambda a: pl.BlockSpec(a.shape, lambda t: (0,) * a.ndim)
    halo_prev = pl.BlockSpec((SUBLANES, CONV_W), lambda t: (jnp.maximum(t * hb - 1, 0), 0))
    halo_next = pl.BlockSpec((SUBLANES, CONV_W), lambda t: (jnp.minimum((t + 1) * hb, nb8 - 1), 0))
    return pl.pallas_call(
        functools.partial(_merge_kernel, tiles_per_batch=tiles_per_batch, seq=seq, n_latent=n_latent),
        grid=(n // tm,),
        in_specs=[row(CONV_W), row(CONV_W), halo_prev, halo_next, row(SGU_W), row(NA_W), row(d),
                  full(mod), full(conv_w), full(out_g), full(w_out), full(n2g)],
        out_specs=[row(d), row(d)],
        out_shape=[jax.ShapeDtypeStruct((n, d), F32), jax.ShapeDtypeStruct((n, d), MXU_DTYPE)],
        compiler_params=_cparams(("parallel",)),
    )(gb, z, z, z, ysgu, yna, xa, mod, conv_w, out_g, w_out, n2g)


def _odd_even_merge(lo, hi, r):
    step = r * 2
    if step < hi - lo:
        yield from _odd_even_merge(lo, hi, step)
        yield from _odd_even_merge(lo + r, hi, step)
        for i in range(lo + r, hi - r, step):
            yield (i, i + r)
    else:
        yield (lo, lo + r)


def _odd_even_sort(lo, hi):
    if hi - lo >= 1:
        mid = lo + (hi - lo) // 2
        yield from _odd_even_sort(lo, mid)
        yield from _odd_even_sort(mid + 1, hi)
        yield from _odd_even_merge(lo, hi, 1)


def _sort_desc(vals):
    for i, j in _odd_even_sort(0, len(vals) - 1):
        vals[i], vals[j] = jnp.maximum(vals[i], vals[j]), jnp.minimum(vals[i], vals[j])
    return vals


def _merge_bitonic_desc(vals):
    n = len(vals)
    k = n // 2
    while k >= 1:
        for i in range(n):
            if (i & k) == 0:
                hi, lo = jnp.maximum(vals[i], vals[i + k]), jnp.minimum(vals[i], vals[i + k])
                vals[i], vals[i + k] = hi, lo
        k //= 2
    return vals


def _top_merge(a, b):
    n = len(a)
    return _merge_bitonic_desc([jnp.maximum(a[i], b[n - 1 - i]) for i in range(n)])


def _top16_over_keys(tiles):
    vals = _sort_desc(list(tiles))
    for shift in (4, 2, 1):
        vals = _top_merge(vals, [pltpu.roll(v, shift, 0) for v in vals])
    return vals


def _pair_threshold(a, b):
    k = PEER_TOPK
    rows = [[a[i] + b[j] for j in range(k // (i + 1))] for i in range(k)]
    neg = jnp.full(a[0].shape, -jnp.inf, F32)

    def merge(x, y, n):
        return _merge_bitonic_desc(x + [neg] * (n - len(x) - len(y)) + y[::-1])

    l1 = merge(rows[1], rows[2], k)
    l2 = merge(merge(rows[3], rows[4], 8), merge(merge(rows[5], rows[6], 4), rows[7], 8), k)
    l3 = [r[0] for r in rows[8:]] + [neg] * (k // 2)
    return _top_merge(_top_merge(rows[0], l1), _top_merge(l2, l3))


def _last_true(b, pred):
    p8 = pred(b[7])
    v4 = jnp.where(p8, b[11], b[3])
    p4 = pred(v4)
    v2 = jnp.where(p8, jnp.where(p4, b[13], b[9]), jnp.where(p4, b[5], b[1]))
    p2 = pred(v2)
    lo = jnp.where(p4, jnp.where(p2, b[6], b[4]), jnp.where(p2, b[2], b[0]))
    hi = jnp.where(p4, jnp.where(p2, b[14], b[12]), jnp.where(p2, b[10], b[8]))
    v1 = jnp.where(p8, hi, lo)
    p1 = pred(v1)
    none = jnp.full(b[0].shape, jnp.inf, F32)
    out = jnp.where(p8, b[7], none)
    for p, v in ((p4, v4), (p2, v2), (p1, v1), (pred(b[15]), b[15])):
        out = jnp.where(p, v, out)
    return out


def _gelu_tanh(x):
    k1 = -2.0 * (2.0 / np.pi) ** 0.5
    k2 = k1 * 0.044715
    return x / (1.0 + jnp.exp(x * (k1 + k2 * (x * x))))


def _peer_kernel(zero_ref, h2_ref, x_ref, mod_ref, fg_ref, wq_ref, sk_ref, u_first_ref, u_odd_ref, u_next_ref,
                 vt_ref, o_ref, s_sc, th_sc, e0_sc, e1_sc, th_st, e0_st, h2t_sc, qt_sc,
                 act_e0, act_e1, act_e2, act_e3, act_o0, act_o1, act_o2, act_o3, w_e, w_o, acc_sc,
                 *, tiles_per_batch, final_norm):
    d = D_MODEL
    tm = TOKEN_TILE
    nk = PEER_NKEYS
    key_tiles = nk // SUBLANES
    lane_tiles = tm // LANES
    tile = 2 * SUBLANES
    gate_tiles = nk // tile
    n_pairs = PEER_EXPERT_BLOCK // PEER_PAIR
    e = pl.program_id(1)
    assert PEER_EXPERT_BLOCK == SUBLANES * nk

    @pl.when(e == 0)
    def _scores_and_thresholds():
        h2t_sc[...] = h2_ref[...].astype(F32).T.astype(h2t_sc.dtype)
        for ip, buf in enumerate((act_e0, act_e1, act_e2, act_e3)):
            buf[0] = jnp.dot(u_first_ref[ip * PEER_PAIR:(ip + 1) * PEER_PAIR, :], h2t_sc[...],
                             preferred_element_type=F32)
        q_t = jnp.dot(wq_ref[...], h2t_sc[...], preferred_element_type=F32)
        qt_sc[...] = q_t.astype(qt_sc.dtype)
        for hp in range(2 * PEER_HEADS):
            s_t = jnp.dot(sk_ref[hp], qt_sc[hp * nk:(hp + 1) * nk, :], preferred_element_type=F32)
            for lt in range(lane_tiles):
                s_sc[hp, lt] = s_t[:, lt * LANES:(lt + 1) * LANES]
        acc_sc[...] = jnp.zeros_like(acc_sc)

        def head_tile(it, carry):
            h = it // lane_tiles
            lt = it % lane_tiles
            rows = lambda v: slice(v * SUBLANES, (v + 1) * SUBLANES)
            s0 = [s_sc[2 * h, lt, rows(v), :] for v in range(key_tiles)]
            s1 = [s_sc[2 * h + 1, lt, rows(v), :] for v in range(key_tiles)]
            a = _top16_over_keys(s0)
            b = _top16_over_keys(s1)
            best = _pair_threshold(a, b)
            tau = best[PEER_TOPK - 1]
            z = jnp.ones_like(tau)
            for kk in range(1, PEER_TOPK):
                z = z + jnp.exp(best[kk] - best[0])
            rz = 1.0 / z
            for v in range(key_tiles):
                th_sc[h, lt, rows(v), :] = _last_true(b, lambda bk: s0[v] + bk >= tau)
                e0_sc[h, lt, rows(v), :] = jnp.exp(s0[v] - a[0])
                e1_sc[h, lt, rows(v), :] = jnp.exp(s1[v] - b[0]) * rz
            return carry

        lax.fori_loop(0, PEER_HEADS * lane_tiles, head_tile, 0)

    base = pl.multiple_of(e * 2 * SUBLANES, 2 * SUBLANES)
    for h in range(PEER_HEADS):
        for lt in range(lane_tiles):
            th_st[h, lt] = th_sc[h, lt, pl.ds(base, 2 * SUBLANES), :]
            e0_st[h, lt] = e0_sc[h, lt, pl.ds(base, 2 * SUBLANES), :]
    z0 = zero_ref[0]

    def pre_activation(u_block, bufs, ip):
        bufs[ip][0] = jnp.dot(u_block[ip * PEER_PAIR:(ip + 1) * PEER_PAIR, :], h2t_sc[...],
                              preferred_element_type=F32)

    def gated_activation(s, bufs, w_buf, ip):
        kpp = PEER_KEYS_PER_PASS
        for lt in range(lane_tiles):
            lanes = slice(lt * LANES, (lt + 1) * LANES)
            for i0 in range(0, 2, kpp):
                gate = [[jnp.zeros((tile, LANES), F32) for _ in range(gate_tiles)] for _ in range(kpp)]
                for h in range(PEER_HEADS):
                    def row(ref, ii):
                        r = s * SUBLANES + 2 * ip + i0 + ii
                        return jnp.broadcast_to(ref[h, lt, r:r + 1, :], (tile, LANES))
                    th = [row(th_st, ii) for ii in range(kpp)]
                    e0 = [row(e0_st, ii) for ii in range(kpp)]
                    for jp in range(gate_tiles):
                        jrow = slice(jp * tile, (jp + 1) * tile)
                        s1 = s_sc[2 * h + 1, lt, jrow, :]
                        e1 = e1_sc[h, lt, jrow, :]
                        for ii in range(kpp):
                            gate[ii][jp] = gate[ii][jp] + jnp.where(s1 >= th[ii], e1 * e0[ii], 0.0)
                for ii in range(kpp):
                    for jp in range(gate_tiles):
                        r0 = (i0 + ii) * nk + jp * tile
                        act = _gelu_tanh(bufs[ip][z0, r0:r0 + tile, lanes])
                        w0 = ip * PEER_PAIR + r0
                        w_buf[0, w0:w0 + tile, lanes] = (gate[ii][jp] * act).astype(w_buf.dtype)

    def output(s, w_buf):
        for ic in range(PEER_OUT_CHUNKS):
            rows = slice(ic * (d // PEER_OUT_CHUNKS), (ic + 1) * (d // PEER_OUT_CHUNKS))
            acc_sc[rows, :] += jnp.dot(vt_ref[s, rows, :], w_buf[z0], preferred_element_type=F32)

    even = (act_e0, act_e1, act_e2, act_e3)
    odd = (act_o0, act_o1, act_o2, act_o3)
    for ip in range(n_pairs):
        pre_activation(u_odd_ref, odd, ip)
        gated_activation(0, even, w_e, ip)
    output(0, w_e)
    for ip in range(n_pairs):
        pre_activation(u_next_ref, even, ip)
        gated_activation(1, odd, w_o, ip)
    output(1, w_o)

    @pl.when(e == pl.num_programs(1) - 1)
    def _residual():
        g2 = _mod_row(mod_ref, tiles_per_batch)[:, 5 * d:6 * d]
        y = x_ref[...] + g2 * acc_sc[...].T
        o_ref[...] = _rms(y, fg_ref[...]) if final_norm else y


def _peer(h2, xm, mod, final_g, wq_t, subkeys, u_tab, vt_tab, layer, tiles_per_batch, n, final_norm):
    d = xm.shape[1]
    tm = TOKEN_TILE
    eb = PEER_EXPERT_BLOCK
    nb = u_tab.shape[1]
    full = lambda a: pl.BlockSpec(a.shape, lambda t, e: (0,) * a.ndim)
    row = pl.BlockSpec((tm, d), lambda t, e: (t, 0))
    u_block = lambda blk: pl.BlockSpec((None, None, eb, d), lambda t, e: (layer, blk(e), 0, 0))
    hp2 = 2 * PEER_HEADS
    lt = tm // LANES
    return pl.pallas_call(
        functools.partial(_peer_kernel, tiles_per_batch=tiles_per_batch, final_norm=final_norm),
        grid=(n // tm, nb // 2),
        in_specs=[pl.BlockSpec(memory_space=pltpu.SMEM), row, row, full(mod), full(final_g),
                  pl.BlockSpec((None,) + wq_t.shape[1:], lambda t, e: (layer, 0, 0)), full(subkeys),
                  u_block(lambda e: 0),
                  u_block(lambda e: 2 * e + 1),
                  u_block(lambda e: jnp.minimum(2 * e + 2, nb - 1)),
                  pl.BlockSpec((None, 2, d, eb), lambda t, e: (layer, e, 0, 0))],
        out_specs=row,
        out_shape=jax.ShapeDtypeStruct((n, d), F32),
        scratch_shapes=[pltpu.VMEM((hp2, lt, PEER_NKEYS, LANES), F32),
                        pltpu.VMEM((PEER_HEADS, lt, PEER_NKEYS, LANES), F32),
                        pltpu.VMEM((PEER_HEADS, lt, PEER_NKEYS, LANES), F32),
                        pltpu.VMEM((PEER_HEADS, lt, PEER_NKEYS, LANES), F32),
                        pltpu.VMEM((PEER_HEADS, lt, 2 * SUBLANES, LANES), F32),
                        pltpu.VMEM((PEER_HEADS, lt, 2 * SUBLANES, LANES), F32),
                        pltpu.VMEM((d, tm), MXU_DTYPE),
                        pltpu.VMEM((hp2 * PEER_NKEYS, tm), MXU_DTYPE)]
                       + [pltpu.VMEM((1, PEER_PAIR, tm), F32)] * 8
                       + [pltpu.VMEM((1, eb, tm), MXU_DTYPE)] * 2
                       + [pltpu.VMEM((d, tm), F32)],
        compiler_params=_cparams(("parallel", "arbitrary")),
    )(jnp.zeros((1,), jnp.int32), h2, xm, mod, final_g, wq_t, subkeys, u_tab, u_tab, u_tab, vt_tab)


def kernel(x, c, ctx, c_ctx, ada_w, ada_b, norm1_g, w_in, conv_w, sgu_w, sgu_b, sgu_norm_g, na_rpb,
           out_norm_g, w_out, norm2_g, peer_wq, peer_subkeys, peer_u, peer_v, final_norm_g):
    batch, seq, d = x.shape
    depth = ada_w.shape[0]
    assert d == D_MODEL and batch == 2 and ctx.shape[1] == CTX_LEN
    assert seq % TOKEN_TILE == 0 and batch * CTX_LEN == TOKEN_TILE
    rows = seq // GRID_W
    assert rows % QUERY_ROWS == 0 and rows >= KEY_ROWS + QUERY_ROWS
    n_latent = batch * seq
    tiles_per_batch = seq // TOKEN_TILE

    xa = jnp.concatenate([x.reshape(n_latent, d), ctx.reshape(batch * CTX_LEN, d)], axis=0)
    cvec = jnp.zeros((SUBLANES, d), F32).at[0:batch].set(c).at[batch].set(c_ctx)
    mods = _adaln(cvec, ada_w, ada_b)
    bias = _build_bias(na_rpb, _bias_block_index(rows))

    mx = MXU_DTYPE
    w_in_c = w_in.astype(mx)
    w_out_c = w_out.astype(mx)
    sgu_w_c = sgu_w.astype(mx)
    sgu_bias = jnp.repeat(jnp.swapaxes(sgu_b, 1, 2), GROUP_DIM, axis=2)
    wq_t = jnp.swapaxes(peer_wq, 1, 2).astype(mx)
    subkeys = peer_subkeys.reshape(depth, 2 * PEER_HEADS, PEER_NKEYS, PEER_DKEY // 2).astype(mx)
    n_blocks = peer_u.shape[1] // PEER_EXPERT_BLOCK
    u_c = peer_u.astype(mx).reshape(depth, n_blocks, PEER_EXPERT_BLOCK, d)
    vt_c = jnp.swapaxes(peer_v.astype(mx).reshape(depth, n_blocks, PEER_EXPERT_BLOCK, d), 2, 3)

    for l in range(depth):
        mod = mods[l]
        gb, z, ysgu, qkv = _proj(xa, mod, norm1_g[l][None], w_in_c[l], sgu_w_c[l], sgu_bias[l],
                                 sgu_norm_g[l][None], tiles_per_batch)
        yna = _attention(qkv, bias, l, batch, seq)
        xm, h2 = _merge(gb, z, ysgu, yna, xa, mod, conv_w[l], out_norm_g[l][None], w_out_c[l],
                        norm2_g[l][None], tiles_per_batch, seq, n_latent)
        last = l == depth - 1
        xa = _peer(h2, xm, mod, final_norm_g[None], wq_t, subkeys[l], u_c, vt_c, l, tiles_per_batch,
                   n_latent if last else xm.shape[0], last)
    return xa.reshape(batch, seq, d)
```

```python
import functools

import numpy as np
import jax
import jax.numpy as jnp
from jax import lax
from jax.experimental import pallas as pl
from jax.experimental.pallas import tpu as pltpu

D_MODEL = 1024
GRID_W = 64
CTX_LEN = 256
EPS = 1e-6
GROUP_DIM = 64
CONV_W = 256
SGU_W = 256
SGU_GROUPS = 4
CHUNK = 128
NA_W = 512
NA_KH = 8
NA_KW = 16
MIX_COLS = 3 * CONV_W + 2 * SGU_W
PEER_HEADS = 8
PEER_NKEYS = 128
PEER_TOPK = 16
PEER_DKEY = 256

LANES = 128
SUBLANES = 8
TOKEN_TILE = 512
QUERY_ROWS = 4
QUERY_BLOCK = QUERY_ROWS * GRID_W
KEY_ROWS = 12
KEY_BLOCK = KEY_ROWS * GRID_W
MASKED = -1e30
PEER_EXPERT_BLOCK = 1024
PEER_PAIR = 2 * PEER_NKEYS
PEER_KEYS_PER_PASS = 2
VMEM_LIMIT = 60 * 1024 * 1024

MXU_DTYPE = jnp.bfloat16
F32 = jnp.float32
NT_DIMS = (((1,), (1,)), ((), ()))


def _cparams(sem):
    return pltpu.CompilerParams(dimension_semantics=sem, vmem_limit_bytes=VMEM_LIMIT)


def _rms(x, g):
    ms = jnp.mean(x * x, axis=-1, keepdims=True)
    return x * lax.rsqrt(ms + EPS) * g


def _group_rms128(y, g, lo):
    y2 = y * y
    s_lo = jnp.sum(jnp.where(lo, y2, 0.0), axis=-1, keepdims=True)
    s_hi = jnp.sum(jnp.where(lo, 0.0, y2), axis=-1, keepdims=True)
    inv = 1.0 / GROUP_DIM
    r = jnp.where(lo, lax.rsqrt(s_lo * inv + EPS), lax.rsqrt(s_hi * inv + EPS))
    return y * r * g


def _lo_mask():
    return lax.broadcasted_iota(jnp.int32, (1, LANES), 1) < GROUP_DIM


def _mod_row(mod_ref, tiles_per_batch):
    g = jnp.minimum(pl.program_id(0) // tiles_per_batch, 2)
    return mod_ref[pl.ds(g, 1), :]


def _adaln_kernel(c_ref, w_ref, b_ref, o_ref):
    cv = c_ref[...]
    s = cv * (1.0 / (1.0 + jnp.exp(-cv)))
    o_ref[...] = jnp.dot(s.astype(MXU_DTYPE), w_ref[...].astype(MXU_DTYPE),
                         preferred_element_type=F32) + b_ref[...]


def _adaln(cvec, ada_w, ada_b):
    depth, d, cols = ada_w.shape
    cb = 1536
    return pl.pallas_call(
        _adaln_kernel,
        grid=(depth, cols // cb),
        in_specs=[pl.BlockSpec((SUBLANES, d), lambda l, j: (0, 0)),
                  pl.BlockSpec((None, d, cb), lambda l, j: (l, 0, j)),
                  pl.BlockSpec((None, 1, cb), lambda l, j: (l, 0, j))],
        out_specs=pl.BlockSpec((None, SUBLANES, cb), lambda l, j: (l, 0, j)),
        out_shape=jax.ShapeDtypeStruct((depth, SUBLANES, cols), F32),
        compiler_params=_cparams(("parallel", "parallel")),
    )(cvec, ada_w, ada_b.reshape(depth, 1, cols))


def _proj_kernel(x_ref, mod_ref, n1g_ref, w_ref, sw_ref, sb_ref, sg_ref,
                 gb_ref, z_ref, ysgu_ref, qkv_ref, *, tiles_per_batch):
    d = D_MODEL
    mrow = _mod_row(mod_ref, tiles_per_batch)
    sh1, sc1 = mrow[:, 0:d], mrow[:, d:2 * d]
    h = _rms(x_ref[...], n1g_ref[...]) * (1.0 + sc1) + sh1
    p = jnp.dot(h.astype(MXU_DTYPE), w_ref[...], preferred_element_type=F32)
    gb_ref[...] = p[:, 0:CONV_W]
    z_ref[...] = p[:, CONV_W:2 * CONV_W] * p[:, 2 * CONV_W:3 * CONV_W]
    sg = jax.nn.gelu(p[:, 3 * CONV_W:MIX_COLS])
    u, v = sg[:, :SGU_W], sg[:, SGU_W:]
    lo = _lo_mask()
    vn = jnp.concatenate(
        [_group_rms128(v[:, :LANES], sg_ref[:, :LANES], lo),
         _group_rms128(v[:, LANES:], sg_ref[:, LANES:], lo)], axis=1).astype(MXU_DTYPE)
    grp = lax.broadcasted_iota(jnp.int32, (1, SGU_W), 1) // GROUP_DIM
    for c in range(TOKEN_TILE // CHUNK):
        rows = slice(c * CHUNK, (c + 1) * CHUNK)
        vc = vn[rows, :]
        mixed = jnp.zeros((CHUNK, SGU_W), F32)
        for gi in range(SGU_GROUPS):
            mg = jnp.dot(sw_ref[gi], vc, preferred_element_type=F32)
            mixed = jnp.where(grp == gi, mg, mixed)
        ysgu_ref[rows, :] = u[rows, :] * (mixed + sb_ref[...])
    qkv_ref[...] = p[:, MIX_COLS:].astype(qkv_ref.dtype)


def _proj(xa, mod, n1g, w_in, sgu_w, sgu_bias, sgu_g, tiles_per_batch):
    n, d = xa.shape
    tm = TOKEN_TILE
    row = lambda w: pl.BlockSpec((tm, w), lambda t: (t, 0))
    full = lambda a: pl.BlockSpec(a.shape, lambda t: (0,) * a.ndim)
    return pl.pallas_call(
        functools.partial(_proj_kernel, tiles_per_batch=tiles_per_batch),
        grid=(n // tm,),
        in_specs=[row(d), full(mod), full(n1g), full(w_in), full(sgu_w), full(sgu_bias), full(sgu_g)],
        out_specs=[row(CONV_W), row(CONV_W), row(SGU_W), row(3 * NA_W)],
        out_shape=[jax.ShapeDtypeStruct((n, CONV_W), F32), jax.ShapeDtypeStruct((n, CONV_W), F32),
                   jax.ShapeDtypeStruct((n, SGU_W), F32), jax.ShapeDtypeStruct((n, 3 * NA_W), MXU_DTYPE)],
        compiler_params=_cparams(("parallel",)),
    )(xa, mod, n1g, w_in, sgu_w, sgu_bias, sgu_g)


def _attn_kernel(q_ref, k_ref, v_ref, kc_ref, vc_ref, bias_ref, o_ref, *, rows):
    j = pl.program_id(2)
    kr0 = jnp.clip(QUERY_ROWS * j - NA_KH // 2, 0, rows - KEY_ROWS)
    start = pl.multiple_of(kr0 * GRID_W, GRID_W)
    kl = k_ref[pl.ds(start, KEY_BLOCK), :]
    vl = v_ref[pl.ds(start, KEY_BLOCK), :]
    kc, vc, q = kc_ref[...], vc_ref[...], q_ref[...]
    lo = _lo_mask()
    scale = GROUP_DIM ** -0.5
    o = jnp.zeros((QUERY_BLOCK, LANES), F32)
    for hh in range(2):
        hm = lo if hh == 0 else jnp.logical_not(lo)
        qh = jnp.where(hm, q, jnp.zeros_like(q)) * scale
        s_loc = lax.dot_general(qh, kl, NT_DIMS, preferred_element_type=F32) + bias_ref[0, hh]
        s_ctx = lax.dot_general(qh, kc, NT_DIMS, preferred_element_type=F32)
        m = jnp.maximum(jnp.max(s_loc, axis=-1, keepdims=True), jnp.max(s_ctx, axis=-1, keepdims=True))
        p_loc = jnp.exp(s_loc - m)
        p_ctx = jnp.exp(s_ctx - m)
        den = jnp.sum(p_loc, axis=-1, keepdims=True) + jnp.sum(p_ctx, axis=-1, keepdims=True)
        vlh = jnp.where(hm, vl, jnp.zeros_like(vl))
        vch = jnp.where(hm, vc, jnp.zeros_like(vc))
        oh = (jnp.dot(p_loc.astype(MXU_DTYPE), vlh, preferred_element_type=F32)
              + jnp.dot(p_ctx.astype(MXU_DTYPE), vch, preferred_element_type=F32))
        o = o + oh * (1.0 / den)
    o_ref[...] = o


def _attention(qkv, bias, layer, batch, seq):
    n = qkv.shape[0]
    rows = seq // GRID_W
    nblk = rows // QUERY_ROWS
    ctx_q0 = batch * seq // QUERY_BLOCK
    heads_cols = NA_W // LANES

    def q_map(b, hp, j):
        return (jnp.where(j < nblk, b * nblk + j, ctx_q0 + b), hp)

    def variant(b, hp, j):
        v = jnp.where(j == 0, 0, jnp.where(j < nblk - 1, 1, jnp.where(j == nblk - 1, 2, 3)))
        return (layer, v, hp, 0, 0)

    return pl.pallas_call(
        functools.partial(_attn_kernel, rows=rows),
        grid=(batch, heads_cols, nblk + 1),
        in_specs=[pl.BlockSpec((QUERY_BLOCK, LANES), q_map),
                  pl.BlockSpec((seq, LANES), lambda b, hp, j: (b, heads_cols + hp)),
                  pl.BlockSpec((seq, LANES), lambda b, hp, j: (b, 2 * heads_cols + hp)),
                  pl.BlockSpec((CTX_LEN, LANES), lambda b, hp, j: (ctx_q0 + b, heads_cols + hp)),
                  pl.BlockSpec((CTX_LEN, LANES), lambda b, hp, j: (ctx_q0 + b, 2 * heads_cols + hp)),
                  pl.BlockSpec((None, 1, 2, QUERY_BLOCK, KEY_BLOCK), variant)],
        out_specs=pl.BlockSpec((QUERY_BLOCK, LANES), q_map),
        out_shape=jax.ShapeDtypeStruct((n, NA_W), F32),
        compiler_params=_cparams(("parallel", "parallel", "arbitrary")),
    )(qkv, qkv, qkv, qkv, qkv, bias)


def _bias_block_index(rows):
    kh = min(NA_KH, rows)
    idx = np.full((4, QUERY_ROWS, KEY_ROWS), 2 * NA_KH - 1, np.int32)
    for v, (r0, kr0) in enumerate(((0, 0), (QUERY_ROWS, 0), (rows - QUERY_ROWS, rows - KEY_ROWS))):
        for qr in range(QUERY_ROWS):
            r = r0 + qr
            rs = min(max(r - kh // 2, 0), rows - kh)
            for kr in range(KEY_ROWS):
                if rs <= kr0 + kr < rs + kh:
                    idx[v, qr, kr] = kr0 + kr - r + NA_KH - 1
    return idx


def _build_bias(rpb, block_index):
    depth, heads = rpb.shape[:2]
    pad = GRID_W - NA_KW
    padded = jnp.pad(rpb, ((0, 0), (0, 0), (0, 0), (pad, pad)), constant_values=MASKED)
    toe = jnp.stack([padded[..., GRID_W - 1 - qc:2 * GRID_W - 1 - qc] for qc in range(GRID_W)], axis=3)
    qc = np.arange(GRID_W)[:, None]
    kc = np.arange(GRID_W)[None, :]
    cs = np.clip(qc - NA_KW // 2, 0, GRID_W - NA_KW)
    toe = jnp.where((kc >= cs) & (kc < cs + NA_KW), toe, MASKED)
    toe = jnp.concatenate([toe, jnp.full((depth, heads, 1, GRID_W, GRID_W), MASKED, toe.dtype)], axis=2)
    flat = [int(a) for a in block_index.reshape(-1)]
    blocks = jnp.stack([toe[:, :, a] for a in flat], axis=2)
    blocks = blocks.reshape(depth, heads, 4, QUERY_ROWS, KEY_ROWS, GRID_W, GRID_W)
    return jnp.transpose(blocks, (0, 2, 1, 3, 5, 4, 6)).reshape(depth, 4, heads, QUERY_BLOCK, KEY_BLOCK)


def _merge_kernel(gb_ref, z_ref, zp_ref, zn_ref, ysgu_ref, yna_ref, x_ref, mod_ref, cw_ref, og_ref,
                  w_ref, n2g_ref, xo_ref, h2_ref, *, tiles_per_batch, seq, n_latent):
    d = D_MODEL
    tm = TOKEN_TILE
    t = pl.program_id(0)
    mrow = _mod_row(mod_ref, tiles_per_batch)
    g1, sh2, sc2 = mrow[:, 2 * d:3 * d], mrow[:, 3 * d:4 * d], mrow[:, 4 * d:5 * d]
    z = z_ref[...]
    ridx = lax.broadcasted_iota(jnp.int32, (tm, 1), 0)
    grow = t * tm + ridx
    seqlen = jnp.where(t * tm >= n_latent, CTX_LEN, seq)
    pos = grow & (seqlen - 1)
    z_prev = jnp.where(ridx == 0, zp_ref[SUBLANES - 1:SUBLANES, :], pltpu.roll(z, 1, 0))
    z_next = jnp.where(ridx == tm - 1, zn_ref[0:1, :], pltpu.roll(z, tm - 1, 0))
    z_prev = jnp.where(pos == 0, 0.0, z_prev)
    z_next = jnp.where(pos == seqlen - 1, 0.0, z_next)
    conv = z_prev * cw_ref[0:1, :] + z * cw_ref[1:2, :] + z_next * cw_ref[2:3, :]
    y_conv = gb_ref[...] * conv
    lo = _lo_mask()
    blocks = []
    for src, width in ((y_conv, CONV_W), (ysgu_ref[...], SGU_W), (yna_ref[...], NA_W)):
        for c in range(width // LANES):
            off = len(blocks) * LANES
            blocks.append(_group_rms128(src[:, c * LANES:(c + 1) * LANES], og_ref[:, off:off + LANES], lo)
                          .astype(MXU_DTYPE))
    y = jnp.dot(jnp.concatenate(blocks, axis=1), w_ref[...], preferred_element_type=F32)
    xm = x_ref[...] + g1 * y
    xo_ref[...] = xm
    h2_ref[...] = (_rms(xm, n2g_ref[...]) * (1.0 + sc2) + sh2).astype(h2_ref.dtype)


def _merge(gb, z, ysgu, yna, xa, mod, conv_w, out_g, w_out, n2g, tiles_per_batch, seq, n_latent):
    n, d = xa.shape
    tm = TOKEN_TILE
    hb = tm // SUBLANES
    nb8 = n // SUBLANES
    row = lambda w: pl.BlockSpec((tm, w), lambda t: (t, 0))
    full = lambda a: pl.BlockSpec(a.shape, lambda t: (0,) * a.ndim)
    halo_prev = pl.BlockSpec((SUBLANES, CONV_W), lambda t: (jnp.maximum(t * hb - 1, 0), 0))
    halo_next = pl.BlockSpec((SUBLANES, CONV_W), lambda t: (jnp.minimum((t + 1) * hb, nb8 - 1), 0))
    return pl.pallas_call(
        functools.partial(_merge_kernel, tiles_per_batch=tiles_per_batch, seq=seq, n_latent=n_latent),
        grid=(n // tm,),
        in_specs=[row(CONV_W), row(CONV_W), halo_prev, halo_next, row(SGU_W), row(NA_W), row(d),
                  full(mod), full(conv_w), full(out_g), full(w_out), full(n2g)],
        out_specs=[row(d), row(d)],
        out_shape=[jax.ShapeDtypeStruct((n, d), F32), jax.ShapeDtypeStruct((n, d), MXU_DTYPE)],
        compiler_params=_cparams(("parallel",)),
    )(gb, z, z, z, ysgu, yna, xa, mod, conv_w, out_g, w_out, n2g)


def _odd_even_merge(lo, hi, r):
    step = r * 2
    if step < hi - lo:
        yield from _odd_even_merge(lo, hi, step)
        yield from _odd_even_merge(lo + r, hi, step)
        for i in range(lo + r, hi - r, step):
            yield (i, i + r)
    else:
        yield (lo, lo + r)


def _odd_even_sort(lo, hi):
    if hi - lo >= 1:
        mid = lo + (hi - lo) // 2
        yield from _odd_even_sort(lo, mid)
        yield from _odd_even_sort(mid + 1, hi)
        yield from _odd_even_merge(lo, hi, 1)


def _sort_desc(vals):
    for i, j in _odd_even_sort(0, len(vals) - 1):
        vals[i], vals[j] = jnp.maximum(vals[i], vals[j]), jnp.minimum(vals[i], vals[j])
    return vals


def _merge_bitonic_desc(vals):
    n = len(vals)
    k = n // 2
    while k >= 1:
        for i in range(n):
            if (i & k) == 0:
                hi, lo = jnp.maximum(vals[i], vals[i + k]), jnp.minimum(vals[i], vals[i + k])
                vals[i], vals[i + k] = hi, lo
        k //= 2
    return vals


def _top_merge(a, b):
    n = len(a)
    return _merge_bitonic_desc([jnp.maximum(a[i], b[n - 1 - i]) for i in range(n)])


def _top16_over_keys(tiles):
    vals = _sort_desc(list(tiles))
    for shift in (4, 2, 1):
        vals = _top_merge(vals, [pltpu.roll(v, shift, 0) for v in vals])
    return vals


def _pair_threshold(a, b):
    k = PEER_TOPK
    rows = [[a[i] + b[j] for j in range(k // (i + 1))] for i in range(k)]
    neg = jnp.full(a[0].shape, -jnp.inf, F32)

    def merge(x, y, n):
        return _merge_bitonic_desc(x + [neg] * (n - len(x) - len(y)) + y[::-1])

    l1 = merge(rows[1], rows[2], k)
    l2 = merge(merge(rows[3], rows[4], 8), merge(merge(rows[5], rows[6], 4), rows[7], 8), k)
    l3 = [r[0] for r in rows[8:]] + [neg] * (k // 2)
    return _top_merge(_top_merge(rows[0], l1), _top_merge(l2, l3))


def _last_true(b, pred):
    p8 = pred(b[7])
    v4 = jnp.where(p8, b[11], b[3])
    p4 = pred(v4)
    v2 = jnp.where(p8, jnp.where(p4, b[13], b[9]), jnp.where(p4, b[5], b[1]))
    p2 = pred(v2)
    lo = jnp.where(p4, jnp.where(p2, b[6], b[4]), jnp.where(p2, b[2], b[0]))
    hi = jnp.where(p4, jnp.where(p2, b[14], b[12]), jnp.where(p2, b[10], b[8]))
    v1 = jnp.where(p8, hi, lo)
    p1 = pred(v1)
    none = jnp.full(b[0].shape, jnp.inf, F32)
    out = jnp.where(p8, b[7], none)
    for p, v in ((p4, v4), (p2, v2), (p1, v1), (pred(b[15]), b[15])):
        out = jnp.where(p, v, out)
    return out


def _gelu_tanh(x):
    k1 = -2.0 * (2.0 / np.pi) ** 0.5
    k2 = k1 * 0.044715
    return x / (1.0 + jnp.exp(x * (k1 + k2 * (x * x))))


def _peer_kernel(zero_ref, h2_ref, x_ref, mod_ref, fg_ref, wq_ref, sk_ref, u_first_ref, u_odd_ref, u_next_ref,
                 vt_ref, o_ref, s_sc, th_sc, e0_sc, e1_sc, th_st, e0_st, h2t_sc, qt_sc,
                 act_e0, act_e1, act_e2, act_e3, act_o0, act_o1, act_o2, act_o3, w_e, w_o, acc_sc,
                 *, tiles_per_batch, final_norm):
    d = D_MODEL
    tm = TOKEN_TILE
    nk = PEER_NKEYS
    key_tiles = nk // SUBLANES
    lane_tiles = tm // LANES
    tile = 2 * SUBLANES
    gate_tiles = nk // tile
    n_pairs = PEER_EXPERT_BLOCK // PEER_PAIR
    e = pl.program_id(1)
    assert PEER_EXPERT_BLOCK == SUBLANES * nk

    @pl.when(e == 0)
    def _scores_and_thresholds():
        h2t_sc[...] = h2_ref[...].astype(F32).T.astype(h2t_sc.dtype)
        for ip, buf in enumerate((act_e0, act_e1, act_e2, act_e3)):
            buf[0] = jnp.dot(u_first_ref[ip * PEER_PAIR:(ip + 1) * PEER_PAIR, :], h2t_sc[...],
                             preferred_element_type=F32)
        q_t = jnp.dot(wq_ref[...], h2t_sc[...], preferred_element_type=F32)
        qt_sc[...] = q_t.astype(qt_sc.dtype)
        for hp in range(2 * PEER_HEADS):
            s_t = jnp.dot(sk_ref[hp], qt_sc[hp * nk:(hp + 1) * nk, :], preferred_element_type=F32)
            for lt in range(lane_tiles):
                s_sc[hp, lt] = s_t[:, lt * LANES:(lt + 1) * LANES]
        acc_sc[...] = jnp.zeros_like(acc_sc)

        def head_tile(it, carry):
            h = it // lane_tiles
            lt = it % lane_tiles
            rows = lambda v: slice(v * SUBLANES, (v + 1) * SUBLANES)
            s0 = [s_sc[2 * h, lt, rows(v), :] for v in range(key_tiles)]
            s1 = [s_sc[2 * h + 1, lt, rows(v), :] for v in range(key_tiles)]
            a = _top16_over_keys(s0)
            b = _top16_over_keys(s1)
            best = _pair_threshold(a, b)
            tau = best[PEER_TOPK - 1]
            z = jnp.ones_like(tau)
            for kk in range(1, PEER_TOPK):
                z = z + jnp.exp(best[kk] - best[0])
            rz = 1.0 / z
            for v in range(key_tiles):
                th_sc[h, lt, rows(v), :] = _last_true(b, lambda bk: s0[v] + bk >= tau)
                e0_sc[h, lt, rows(v), :] = jnp.exp(s0[v] - a[0])
                e1_sc[h, lt, rows(v), :] = jnp.exp(s1[v] - b[0]) * rz
            return carry

        lax.fori_loop(0, PEER_HEADS * lane_tiles, head_tile, 0)

    base = pl.multiple_of(e * 2 * SUBLANES, 2 * SUBLANES)
    for h in range(PEER_HEADS):
        for lt in range(lane_tiles):
            th_st[h, lt] = th_sc[h, lt, pl.ds(base, 2 * SUBLANES), :]
            e0_st[h, lt] = e0_sc[h, lt, pl.ds(base, 2 * SUBLANES), :]
    z0 = zero_ref[0]

    def pre_activation(u_block, bufs, ip):
        bufs[ip][0] = jnp.dot(u_block[ip * PEER_PAIR:(ip + 1) * PEER_PAIR, :], h2t_sc[...],
                              preferred_element_type=F32)

    def gated_activation(s, bufs, w_buf, ip):
        kpp = PEER_KEYS_PER_PASS
        for lt in range(lane_tiles):
            lanes = slice(lt * LANES, (lt + 1) * LANES)
            for i0 in range(0, 2, kpp):
                gate = [[jnp.zeros((tile, LANES), F32) for _ in range(gate_tiles)] for _ in range(kpp)]
                for h in range(PEER_HEADS):
                    def row(ref, ii):
                        r = s * SUBLANES + 2 * ip + i0 + ii
                        return jnp.broadcast_to(ref[h, lt, r:r + 1, :], (tile, LANES))
                    th = [row(th_st, ii) for ii in range(kpp)]
                    e0 = [row(e0_st, ii) for ii in range(kpp)]
                    for jp in range(gate_tiles):
                        jrow = slice(jp * tile, (jp + 1) * tile)
                        s1 = s_sc[2 * h + 1, lt, jrow, :]
                        e1 = e1_sc[h, lt, jrow, :]
                        for ii in range(kpp):
                            gate[ii][jp] = gate[ii][jp] + jnp.where(s1 >= th[ii], e1 * e0[ii], 0.0)
                for ii in range(kpp):
                    for jp in range(gate_tiles):
                        r0 = (i0 + ii) * nk + jp * tile
                        act = _gelu_tanh(bufs[ip][z0, r0:r0 + tile, lanes])
                        w0 = ip * PEER_PAIR + r0
                        w_buf[0, w0:w0 + tile, lanes] = (gate[ii][jp] * act).astype(w_buf.dtype)

    def output(s, w_buf):
        acc_sc[...] += jnp.dot(vt_ref[s], w_buf[z0], preferred_element_type=F32)

    even = (act_e0, act_e1, act_e2, act_e3)
    odd = (act_o0, act_o1, act_o2, act_o3)
    for ip in range(n_pairs):
        pre_activation(u_odd_ref, odd, ip)
        gated_activation(0, even, w_e, ip)
    output(0, w_e)
    for ip in range(n_pairs):
        pre_activation(u_next_ref, even, ip)
        gated_activation(1, odd, w_o, ip)
    output(1, w_o)

    @pl.when(e == pl.num_programs(1) - 1)
    def _residual():
        g2 = _mod_row(mod_ref, tiles_per_batch)[:, 5 * d:6 * d]
        y = x_ref[...] + g2 * acc_sc[...].T
        o_ref[...] = _rms(y, fg_ref[...]) if final_norm else y


def _peer(h2, xm, mod, final_g, wq_t, subkeys, u_tab, vt_tab, layer, tiles_per_batch, n, final_norm):
    d = xm.shape[1]
    tm = TOKEN_TILE
    eb = PEER_EXPERT_BLOCK
    nb = u_tab.shape[1]
    full = lambda a: pl.BlockSpec(a.shape, lambda t, e: (0,) * a.ndim)
    row = pl.BlockSpec((tm, d), lambda t, e: (t, 0))
    u_block = lambda blk: pl.BlockSpec((None, None, eb, d), lambda t, e: (layer, blk(e), 0, 0))
    hp2 = 2 * PEER_HEADS
    lt = tm // LANES
    return pl.pallas_call(
        functools.partial(_peer_kernel, tiles_per_batch=tiles_per_batch, final_norm=final_norm),
        grid=(n // tm, nb // 2),
        in_specs=[pl.BlockSpec(memory_space=pltpu.SMEM), row, row, full(mod), full(final_g),
                  pl.BlockSpec((None,) + wq_t.shape[1:], lambda t, e: (layer, 0, 0)), full(subkeys),
                  u_block(lambda e: 0),
                  u_block(lambda e: 2 * e + 1),
                  u_block(lambda e: jnp.minimum(2 * e + 2, nb - 1)),
                  pl.BlockSpec((None, 2, d, eb), lambda t, e: (layer, e, 0, 0))],
        out_specs=row,
        out_shape=jax.ShapeDtypeStruct((n, d), F32),
        scratch_shapes=[pltpu.VMEM((hp2, lt, PEER_NKEYS, LANES), F32),
                        pltpu.VMEM((PEER_HEADS, lt, PEER_NKEYS, LANES), F32),
                        pltpu.VMEM((PEER_HEADS, lt, PEER_NKEYS, LANES), F32),
                        pltpu.VMEM((PEER_HEADS, lt, PEER_NKEYS, LANES), F32),
                        pltpu.VMEM((PEER_HEADS, lt, 2 * SUBLANES, LANES), F32),
                        pltpu.VMEM((PEER_HEADS, lt, 2 * SUBLANES, LANES), F32),
                        pltpu.VMEM((d, tm), MXU_DTYPE),
                        pltpu.VMEM((hp2 * PEER_NKEYS, tm), MXU_DTYPE)]
                       + [pltpu.VMEM((1, PEER_PAIR, tm), F32)] * 8
                       + [pltpu.VMEM((1, eb, tm), MXU_DTYPE)] * 2
                       + [pltpu.VMEM((d, tm), F32)],
        compiler_params=_cparams(("parallel", "arbitrary")),
    )(jnp.zeros((1,), jnp.int32), h2, xm, mod, final_g, wq_t, subkeys, u_tab, u_tab, u_tab, vt_tab)


def kernel(x, c, ctx, c_ctx, ada_w, ada_b, norm1_g, w_in, conv_w, sgu_w, sgu_b, sgu_norm_g, na_rpb,
           out_norm_g, w_out, norm2_g, peer_wq, peer_subkeys, peer_u, peer_v, final_norm_g):
    batch, seq, d = x.shape
    depth = ada_w.shape[0]
    assert d == D_MODEL and batch == 2 and ctx.shape[1] == CTX_LEN
    assert seq % TOKEN_TILE == 0 and batch * CTX_LEN == TOKEN_TILE
    rows = seq // GRID_W
    assert rows % QUERY_ROWS == 0 and rows >= KEY_ROWS + QUERY_ROWS
    n_latent = batch * seq
    tiles_per_batch = seq // TOKEN_TILE

    xa = jnp.concatenate([x.reshape(n_latent, d), ctx.reshape(batch * CTX_LEN, d)], axis=0)
    cvec = jnp.zeros((SUBLANES, d), F32).at[0:batch].set(c).at[batch].set(c_ctx)
    mods = _adaln(cvec, ada_w, ada_b)
    bias = _build_bias(na_rpb, _bias_block_index(rows))

    mx = MXU_DTYPE
    w_in_c = w_in.astype(mx)
    w_out_c = w_out.astype(mx)
    sgu_w_c = sgu_w.astype(mx)
    sgu_bias = jnp.repeat(jnp.swapaxes(sgu_b, 1, 2), GROUP_DIM, axis=2)
    wq_t = jnp.swapaxes(peer_wq, 1, 2).astype(mx)
    subkeys = peer_subkeys.reshape(depth, 2 * PEER_HEADS, PEER_NKEYS, PEER_DKEY // 2).astype(mx)
    n_blocks = peer_u.shape[1] // PEER_EXPERT_BLOCK
    u_c = peer_u.astype(mx).reshape(depth, n_blocks, PEER_EXPERT_BLOCK, d)
    vt_c = jnp.swapaxes(peer_v.astype(mx).reshape(depth, n_blocks, PEER_EXPERT_BLOCK, d), 2, 3)

    for l in range(depth):
        mod = mods[l]
        gb, z, ysgu, qkv = _proj(xa, mod, norm1_g[l][None], w_in_c[l], sgu_w_c[l], sgu_bias[l],
                                 sgu_norm_g[l][None], tiles_per_batch)
        yna = _attention(qkv, bias, l, batch, seq)
        xm, h2 = _merge(gb, z, ysgu, yna, xa, mod, conv_w[l], out_norm_g[l][None], w_out_c[l],
                        norm2_g[l][None], tiles_per_batch, seq, n_latent)
        last = l == depth - 1
        xa = _peer(h2, xm, mod, final_norm_g[None], wq_t, subkeys[l], u_c, vt_c, l, tiles_per_batch,
                   n_latent if last else xm.shape[0], last)
    return xa.reshape(batch, seq, d)
```
